```python
import jax, jax.numpy as jnp
from jax import lax
import numpy as np

D_MODEL = 2048
BATCH = 4
SEQ = 4096
DEPTH = 2

N_MIXERS = 2
CHUNK = 128
EPS = 1e-6
GMLP_DFF = 6 * D_MODEL
GMLP_HALF = GMLP_DFF // 2
GMLP_GROUPS = 8
GMLP_GROUP_DIM = GMLP_HALF // GMLP_GROUPS
RET_HEADS = 8
RET_QK_DIM = D_MODEL
RET_V_DIM = 2 * D_MODEL
RET_HEAD_QK = RET_QK_DIM // RET_HEADS
RET_HEAD_V = RET_V_DIM // RET_HEADS
ROPE_BASE = 10000.0
FFN_DIM = 5632
CONV_WIDTH = 3

N_LAYERS_A = (DEPTH + 1) // 2
N_LAYERS_B = DEPTH // 2

kernel_name = "hybrid_gmlp_retention_convffn"


def _rms(xf):
    return xf * lax.rsqrt(jnp.mean(xf * xf, axis=-1, keepdims=True) + EPS)


def rmsnorm(x, g):
    y = _rms(x.astype(jnp.float32)) * g.astype(jnp.float32)
    return y.astype(x.dtype)


def layernorm(x, g, b):
    xf = x.astype(jnp.float32)
    mu = jnp.mean(xf, axis=-1, keepdims=True)
    xc = xf - mu
    var = jnp.mean(xc * xc, axis=-1, keepdims=True)
    y = xc * lax.rsqrt(var + EPS) * g.astype(jnp.float32) + b.astype(jnp.float32)
    return y.astype(x.dtype)


def gmlp_mixer(h, w_in, ln_g, ln_b, w_s, b_s, w_out):
    B, S, _ = h.shape
    n = S // CHUNK
    z = jax.nn.gelu(h @ w_in, approximate=False)
    u, v = jnp.split(z, 2, axis=-1)
    v = layernorm(v, ln_g, ln_b)
    v = v.reshape(B, n, CHUNK, GMLP_GROUPS, GMLP_GROUP_DIM)
    mask = jnp.tril(jnp.ones((CHUNK, CHUNK), dtype=bool))
    ws = jnp.where(mask[None], w_s, 0).astype(v.dtype)
    mixed = jnp.einsum('gts,bnsgd->bntgd', ws, v) + b_s.T.astype(v.dtype)[None, None, :, :, None]
    gated = u * mixed.reshape(B, S, GMLP_HALF)
    return gated @ w_out


def rotate_every_two(x):
    x1 = x[..., ::2]
    x2 = x[..., 1::2]
    return jnp.stack((-x2, x1), axis=-1).reshape(x.shape)


def retention_mixer(h, positions, w_in, w_out):
    B, S, _ = h.shape
    n = S // CHUNK
    dt = h.dtype
    proj = h @ w_in
    q, k, v, g = jnp.split(proj, [RET_QK_DIM, 2 * RET_QK_DIM, 2 * RET_QK_DIM + RET_V_DIM], axis=-1)
    q = q.reshape(B, S, RET_HEADS, RET_HEAD_QK)
    k = k.reshape(B, S, RET_HEADS, RET_HEAD_QK)
    v = v.reshape(B, S, RET_HEADS, RET_HEAD_V)
    inv_freq = 1.0 / (ROPE_BASE ** jnp.linspace(0.0, 1.0, RET_HEAD_QK // 2, dtype=jnp.float32))
    inv_freq = jnp.repeat(inv_freq, 2)
    ang = positions.astype(jnp.float32)[..., None] * inv_freq
    cos = jnp.cos(ang)[:, :, None, :]
    sin = jnp.sin(ang)[:, :, None, :]
    q = (q * cos + rotate_every_two(q) * sin).astype(dt)
    k = ((k * cos + rotate_every_two(k) * sin) * (RET_HEAD_QK ** -0.5)).astype(dt)
    log_gamma = jnp.log1p(-jnp.exp2(-5.0 - jnp.arange(RET_HEADS, dtype=jnp.float32)))
    idx = jnp.arange(CHUNK, dtype=jnp.float32)
    rel = idx[:, None] - idx[None, :]
    decay_mask = jnp.where(rel[None] >= 0, jnp.exp(jnp.maximum(rel, 0.0)[None] * log_gamma[:, None, None]), 0.0)
    qc = q.reshape(B, n, CHUNK, RET_HEADS, RET_HEAD_QK)
    kc = k.reshape(B, n, CHUNK, RET_HEADS, RET_HEAD_QK)
    vc = v.reshape(B, n, CHUNK, RET_HEADS, RET_HEAD_V)
    scores = jnp.einsum('bnchd,bnshd->bnhcs', qc, kc) * decay_mask.astype(dt)
    intra = jnp.einsum('bnhcs,bnshe->bnche', scores, vc)
    q_decay = jnp.exp((idx[:, None] + 1.0) * log_gamma[None, :])
    k_decay = jnp.exp((CHUNK - 1.0 - idx)[:, None] * log_gamma[None, :])
    chunk_decay = jnp.exp(CHUNK * log_gamma)
    qd = qc * q_decay[:, :, None].astype(dt)
    kd = kc * k_decay[:, :, None].astype(dt)
    xs = (jnp.moveaxis(qd, 1, 0), jnp.moveaxis(kd, 1, 0), jnp.moveaxis(vc, 1, 0))

    def step(state, inp):
        qi, ki, vi = inp
        cross_i = jnp.einsum('bchd,bhde->bche', qi.astype(jnp.float32), state)
        state = state * chunk_decay[None, :, None, None] + jnp.einsum(
            'bchd,bche->bhde', ki.astype(jnp.float32), vi.astype(jnp.float32))
        return state, cross_i.astype(dt)

    state0 = jnp.zeros((B, RET_HEADS, RET_HEAD_QK, RET_HEAD_V), jnp.float32)
    _, cross = lax.scan(step, state0, xs)
    o = intra + jnp.moveaxis(cross, 0, 1)
    o = o.reshape(B, S, RET_HEADS, RET_HEAD_V)
    o = _rms(o.astype(jnp.float32)).astype(dt).reshape(B, S, RET_V_DIM)
    return (jax.nn.silu(g) * o) @ w_out


def conv_ffn(h, w_up, conv_w, conv_b, w_down):
    S = h.shape[1]
    z = h @ w_up
    zp = jnp.pad(z, ((0, 0), (CONV_WIDTH - 1, 0), (0, 0)))
    zc = conv_b.astype(z.dtype)
    for tap in range(CONV_WIDTH):
        zc = zc + conv_w[tap].astype(z.dtype) * zp[:, tap:tap + S]
    gate, up = jnp.split(zc, 2, axis=-1)
    return (jax.nn.silu(gate) * up) @ w_down


def setup_inputs(seed: int = 0) -> dict:
    key = jax.random.key(seed)
    ks = jax.random.split(key, 20)
    f32 = jnp.float32

    def nrm(k, shape, scale):
        return jax.random.normal(k, shape, f32) * scale

    def gain(k, shape):
        return 1.0 + 0.02 * jax.random.normal(k, shape, f32)

    x = jax.random.normal(ks[0], (BATCH, SEQ, D_MODEL), f32)
    positions = jnp.broadcast_to(jnp.arange(SEQ, dtype=jnp.int32), (BATCH, SEQ))
    return {
        "x": x,
        "positions": positions,
        "mix_pre_g": gain(ks[1], (DEPTH, D_MODEL)),
        "mix_post_g": gain(ks[2], (DEPTH, D_MODEL)),
        "gmlp_w_in": nrm(ks[3], (N_LAYERS_A, D_MODEL, GMLP_DFF), D_MODEL ** -0.5),
        "gmlp_ln_g": gain(ks[4], (N_LAYERS_A, GMLP_HALF)),
        "gmlp_ln_b": nrm(ks[5], (N_LAYERS_A, GMLP_HALF), 0.02),
        "gmlp_w_s": nrm(ks[6], (N_LAYERS_A, GMLP_GROUPS, CHUNK, CHUNK), CHUNK ** -0.5),
        "gmlp_b_s": 1.0 + nrm(ks[7], (N_LAYERS_A, GMLP_GROUPS, CHUNK), 0.1),
        "gmlp_w_out": nrm(ks[8], (N_LAYERS_A, GMLP_HALF, D_MODEL), GMLP_HALF ** -0.5),
        "ret_w_in": nrm(ks[9], (N_LAYERS_B, D_MODEL, 2 * RET_QK_DIM + 2 * RET_V_DIM), D_MODEL ** -0.5),
        "ret_w_out": nrm(ks[10], (N_LAYERS_B, RET_V_DIM, D_MODEL), RET_V_DIM ** -0.5),
        "ffn_pre_g": gain(ks[11], (DEPTH, D_MODEL)),
        "ffn_post_g": gain(ks[12], (DEPTH, D_MODEL)),
        "ffn_w_up": nrm(ks[13], (DEPTH, D_MODEL, 2 * FFN_DIM), D_MODEL ** -0.5),
        "ffn_conv_w": nrm(ks[14], (DEPTH, CONV_WIDTH, 2 * FFN_DIM), CONV_WIDTH ** -0.5),
        "ffn_conv_b": nrm(ks[15], (DEPTH, 2 * FFN_DIM), 0.02),
        "ffn_w_down": nrm(ks[16], (DEPTH, FFN_DIM, D_MODEL), FFN_DIM ** -0.5),
    }


def reference(x, positions, mix_pre_g, mix_post_g, gmlp_w_in, gmlp_ln_g, gmlp_ln_b, gmlp_w_s,
              gmlp_b_s, gmlp_w_out, ret_w_in, ret_w_out, ffn_pre_g, ffn_post_g, ffn_w_up,
              ffn_conv_w, ffn_conv_b, ffn_w_down):
    for i in range(DEPTH):
        j = i // N_MIXERS
        hn = rmsnorm(x, mix_pre_g[i])
        if i % N_MIXERS == 0:
            y = gmlp_mixer(hn, gmlp_w_in[j], gmlp_ln_g[j], gmlp_ln_b[j], gmlp_w_s[j],
                           gmlp_b_s[j], gmlp_w_out[j])
        else:
            y = retention_mixer(hn, positions, ret_w_in[j], ret_w_out[j])
        x = x + rmsnorm(y, mix_post_g[i])
        hn = rmsnorm(x, ffn_pre_g[i])
        y = conv_ffn(hn, ffn_w_up[i], ffn_conv_w[i], ffn_conv_b[i], ffn_w_down[i])
        x = x + rmsnorm(y, ffn_post_g[i])
    return x
```

```python
import functools

import jax
import jax.numpy as jnp
import numpy as np
from jax import lax
from jax.experimental import pallas as pl
from jax.experimental.pallas import tpu as pltpu

EPS = 1e-6
CHUNK = 128
GMLP_GROUPS = 8
RET_HEADS = 8
ROPE_BASE = 10000.0
CONV_WIDTH = 3

F32 = jnp.float32
BF16 = jnp.bfloat16

V7X_VMEM_BYTES = 64 * 1024 * 1024
VMEM_LIMIT_BYTES = V7X_VMEM_BYTES - 8 * 1024 * 1024
LANES = 128
SUBLANES = 8


def _params(*semantics):
    return pltpu.CompilerParams(dimension_semantics=semantics, vmem_limit_bytes=VMEM_LIMIT_BYTES)


def _rms(x):
    return x * lax.rsqrt(jnp.mean(x * x, axis=-1, keepdims=True) + EPS)


def _prenorm_kernel(x_ref, g_ref, o_ref):
    o_ref[...] = (_rms(x_ref[...]) * g_ref[...]).astype(o_ref.dtype)


def _prenorm(x, g, *, bm):
    m, d = x.shape
    return pl.pallas_call(
        _prenorm_kernel,
        grid=(m // bm,),
        in_specs=[pl.BlockSpec((bm, d), lambda i: (i, 0)), pl.BlockSpec((1, d), lambda i: (0, 0))],
        out_specs=pl.BlockSpec((bm, d), lambda i: (i, 0)),
        out_shape=jax.ShapeDtypeStruct((m, d), BF16),
        compiler_params=_params("arbitrary"),
        name="prenorm",
    )(x, g.reshape(1, d))


def _lane_partial_sum(y):
    acc = y[:, 0:LANES]
    for c in range(1, y.shape[1] // LANES):
        acc = acc + y[:, c * LANES:(c + 1) * LANES]
    return acc


def _gelu_exact(y):
    return 0.5 * y * (1.0 + lax.erf(y * np.float32(np.sqrt(0.5))))


def _gmlp_in_kernel(h_ref, w_ref, z_ref, mu_ref, rstd_ref, s_ref, ss_ref, *, n_u_tiles, v_width):
    j = pl.program_id(1)
    y = jnp.dot(h_ref[...], w_ref[...], preferred_element_type=F32)
    z = _gelu_exact(y)
    z_ref[...] = z.astype(z_ref.dtype)

    @pl.when(j == n_u_tiles)
    def _():
        s_ref[...] = jnp.zeros_like(s_ref)
        ss_ref[...] = jnp.zeros_like(ss_ref)

    @pl.when(j >= n_u_tiles)
    def _():
        s_ref[...] += _lane_partial_sum(z)
        ss_ref[...] += _lane_partial_sum(z * z)

    @pl.when(j == pl.num_programs(1) - 1)
    def _():
        mu = jnp.sum(s_ref[...], axis=-1, keepdims=True) / v_width
        var = jnp.sum(ss_ref[...], axis=-1, keepdims=True) / v_width - mu * mu
        mu_ref[...] = mu
        rstd_ref[...] = lax.rsqrt(var + EPS)


def _gmlp_in(h, w, *, bm, bn):
    m, d = h.shape
    n = w.shape[1]
    n_tiles = n // bn
    kern = functools.partial(_gmlp_in_kernel, n_u_tiles=n_tiles // 2, v_width=n // 2)
    return pl.pallas_call(
        kern,
        grid=(m // bm, n_tiles),
        in_specs=[pl.BlockSpec((bm, d), lambda i, j: (i, 0)), pl.BlockSpec((d, bn), lambda i, j: (0, j))],
        out_specs=[
            pl.BlockSpec((bm, bn), lambda i, j: (i, j)),
            pl.BlockSpec((bm, 1), lambda i, j: (i, 0)),
            pl.BlockSpec((bm, 1), lambda i, j: (i, 0)),
        ],
        out_shape=[
            jax.ShapeDtypeStruct((m, n), BF16),
            jax.ShapeDtypeStruct((m, 1), F32),
            jax.ShapeDtypeStruct((m, 1), F32),
        ],
        scratch_shapes=[pltpu.VMEM((bm, LANES), F32), pltpu.VMEM((bm, LANES), F32)],
        compiler_params=_params("arbitrary", "arbitrary"),
        name="gmlp_in",
    )(h, w)


def _residual_norm_epilogue(acc, x_ref, pg_ref, ng_ref, xo_ref, ho_ref):
    xn = x_ref[...] + _rms(acc) * pg_ref[...]
    xo_ref[...] = xn
    if ho_ref is not None:
        ho_ref[...] = (_rms(xn) * ng_ref[...]).astype(ho_ref.dtype)


def _down_kernel(a_ref, w_ref, x_ref, pg_ref, ng_ref, *rest, with_next):
    if with_next:
        xo_ref, ho_ref, acc_ref = rest
    else:
        (xo_ref, acc_ref), ho_ref = rest, None
    k = pl.program_id(1)
    part = jnp.dot(a_ref[...], w_ref[...], preferred_element_type=F32)

    @pl.when(k == 0)
    def _():
        acc_ref[...] = part

    @pl.when(k > 0)
    def _():
        acc_ref[...] += part

    @pl.when(k == pl.num_programs(1) - 1)
    def _():
        _residual_norm_epilogue(acc_ref[...], x_ref, pg_ref, ng_ref, xo_ref, ho_ref)


def _down_proj(a, w, x, post_g, next_g, *, bm, tk, with_next=True):
    m, kdim = a.shape
    d = w.shape[1]
    row = pl.BlockSpec((bm, d), lambda i, k: (i, 0))
    gain = pl.BlockSpec((1, d), lambda i, k: (0, 0))
    out_specs = [row, row] if with_next else [row]
    out_shape = [jax.ShapeDtypeStruct((m, d), F32)]
    if with_next:
        out_shape.append(jax.ShapeDtypeStruct((m, d), BF16))
    res = pl.pallas_call(
        functools.partial(_down_kernel, with_next=with_next),
        grid=(m // bm, kdim // tk),
        in_specs=[
            pl.BlockSpec((bm, tk), lambda i, k: (i, k)),
            pl.BlockSpec((tk, d), lambda i, k: (k, 0)),
            row, gain, gain,
        ],
        out_specs=out_specs,
        out_shape=out_shape,
        scratch_shapes=[pltpu.VMEM((bm, d), F32)],
        compiler_params=_params("arbitrary", "arbitrary"),
        name="down_proj",
    )(a, w, x, post_g.reshape(1, d), next_g.reshape(1, d))
    return res if with_next else (res[0], None)


def _gmlp_down_kernel(u_ref, v_ref, mu_ref, rstd_ref, lg_ref, lb_ref, ws_ref, bs_ref, w_ref,
                      x_ref, pg_ref, ng_ref, xo_ref, ho_ref, lhs_ref, acc_ref, *, groups_per_step):
    k = pl.program_id(1)
    bm, tk = u_ref.shape
    gd = tk // groups_per_step
    tril = (lax.broadcasted_iota(jnp.int32, (CHUNK, CHUNK), 0)
            >= lax.broadcasted_iota(jnp.int32, (CHUNK, CHUNK), 1))
    mu = mu_ref[...]
    rstd = rstd_ref[...]
    for gi in range(groups_per_step):
        g = k * groups_per_step + gi
        cols = slice(gi * gd, (gi + 1) * gd)
        ws = jnp.where(tril, ws_ref[g], 0.0).astype(BF16)
        bias = bs_ref[g]
        vn = (v_ref[:, cols].astype(F32) - mu) * rstd * lg_ref[:, cols] + lb_ref[:, cols]
        vn = vn.astype(BF16)
        for c in range(bm // CHUNK):
            rows = slice(c * CHUNK, (c + 1) * CHUNK)
            mixed = jnp.dot(ws, vn[rows], preferred_element_type=F32) + bias
            lhs_ref[rows, cols] = (u_ref[rows, cols].astype(F32) * mixed).astype(BF16)
    part = jnp.dot(lhs_ref[...], w_ref[...], preferred_element_type=F32)

    @pl.when(k == 0)
    def _():
        acc_ref[...] = part

    @pl.when(k > 0)
    def _():
        acc_ref[...] += part

    @pl.when(k == pl.num_programs(1) - 1)
    def _():
        _residual_norm_epilogue(acc_ref[...], x_ref, pg_ref, ng_ref, xo_ref, ho_ref)


def _gmlp_down(z, mu, rstd, ln_g, ln_b, w_s, b_s, w, x, post_g, next_g, *, bm, groups_per_step):
    m = z.shape[0]
    half, d = w.shape
    gd = half // GMLP_GROUPS
    tk = gd * groups_per_step
    nk = half // tk
    row = pl.BlockSpec((bm, d), lambda i, k: (i, 0))
    gain = pl.BlockSpec((1, d), lambda i, k: (0, 0))
    stat = pl.BlockSpec((bm, 1), lambda i, k: (i, 0))
    lnp = pl.BlockSpec((1, tk), lambda i, k: (0, k))
    return pl.pallas_call(
        functools.partial(_gmlp_down_kernel, groups_per_step=groups_per_step),
        grid=(m // bm, nk),
        in_specs=[
            pl.BlockSpec((bm, tk), lambda i, k: (i, k)),
            pl.BlockSpec((bm, tk), lambda i, k: (i, k + nk)),
            stat, stat, lnp, lnp,
            pl.BlockSpec((GMLP_GROUPS, CHUNK, CHUNK), lambda i, k: (0, 0, 0)),
            pl.BlockSpec((GMLP_GROUPS, CHUNK, 1), lambda i, k: (0, 0, 0)),
            pl.BlockSpec((tk, d), lambda i, k: (k, 0)),
            row, gain, gain,
        ],
        out_specs=[row, row],
        out_shape=[jax.ShapeDtypeStruct((m, d), F32), jax.ShapeDtypeStruct((m, d), BF16)],
        scratch_shapes=[pltpu.VMEM((bm, tk), BF16), pltpu.VMEM((bm, d), F32)],
        compiler_params=_params("arbitrary", "arbitrary"),
        name="gmlp_down",
    )(z, z, mu, rstd, ln_g.reshape(1, half), ln_b.reshape(1, half), w_s,
      b_s.reshape(GMLP_GROUPS, CHUNK, 1), w, x, post_g.reshape(1, d), next_g.reshape(1, d))


def _causal_conv(z, prev, cw, cb):
    row = lax.broadcasted_iota(jnp.int32, z.shape, 0)
    z1 = jnp.where(row == 0, prev[7:8, :], pltpu.roll(z, 1, 0))
    z2 = jnp.where(row == 0, prev[6:7, :], jnp.where(row == 1, prev[7:8, :], pltpu.roll(z, 2, 0)))
    return cb + cw[0:1, :] * z2 + cw[1:2, :] * z1 + cw[2:3, :] * z


def _ffn_up_kernel(h_ref, wg_ref, wu_ref, cwg_ref, cwu_ref, cbg_ref, cbu_ref, o_ref,
                   pg_ref, pu_ref, *, tiles_per_seq):
    i = pl.program_id(1)

    @pl.when(i % tiles_per_seq == 0)
    def _():
        pg_ref[...] = jnp.zeros_like(pg_ref)
        pu_ref[...] = jnp.zeros_like(pu_ref)

    h = h_ref[...]
    bm = h.shape[0]
    zg = jnp.dot(h, wg_ref[...], preferred_element_type=F32)
    gate = _causal_conv(zg, pg_ref[...], cwg_ref[...], cbg_ref[...])
    pg_ref[...] = zg[bm - SUBLANES:, :]
    zu = jnp.dot(h, wu_ref[...], preferred_element_type=F32)
    up = _causal_conv(zu, pu_ref[...], cwu_ref[...], cbu_ref[...])
    pu_ref[...] = zu[bm - SUBLANES:, :]
    o_ref[...] = (jax.nn.silu(gate) * up).astype(o_ref.dtype)


def _ffn_up(h, w, conv_w, conv_b, *, seq, bm, tn):
    m, d = h.shape
    f = w.shape[1] // 2
    nj = f // tn
    return pl.pallas_call(
        functools.partial(_ffn_up_kernel, tiles_per_seq=seq // bm),
        grid=(nj, m // bm),
        in_specs=[
            pl.BlockSpec((bm, d), lambda j, i: (i, 0)),
            pl.BlockSpec((d, tn), lambda j, i: (0, j)),
            pl.BlockSpec((d, tn), lambda j, i: (0, j + nj)),
            pl.BlockSpec((CONV_WIDTH, tn), lambda j, i: (0, j)),
            pl.BlockSpec((CONV_WIDTH, tn), lambda j, i: (0, j + nj)),
            pl.BlockSpec((1, tn), lambda j, i: (0, j)),
            pl.BlockSpec((1, tn), lambda j, i: (0, j + nj)),
        ],
        out_specs=pl.BlockSpec((bm, tn), lambda j, i: (i, j)),
        out_shape=jax.ShapeDtypeStruct((m, f), BF16),
        scratch_shapes=[pltpu.VMEM((SUBLANES, tn), F32), pltpu.VMEM((SUBLANES, tn), F32)],
        compiler_params=_params("arbitrary", "arbitrary"),
        name="ffn_up",
    )(h, w, w, conv_w, conv_w, conv_b.reshape(1, 2 * f), conv_b.reshape(1, 2 * f))


def _rope_table_kernel(pos_ref, inv_ref, cos_ref, sin_ref):
    ang = pos_ref[...].astype(F32) * inv_ref[...]
    cos_ref[...] = jnp.cos(ang)
    sin_ref[...] = jnp.sin(ang)


def _rope_tables(pos, inv_freq, *, bm):
    m = pos.shape[0]
    dk = inv_freq.shape[1]
    tab = pl.BlockSpec((bm, dk), lambda i: (i, 0))
    return pl.pallas_call(
        _rope_table_kernel,
        grid=(m // bm,),
        in_specs=[pl.BlockSpec((bm, 1), lambda i: (i, 0)), pl.BlockSpec((1, dk), lambda i: (0, 0))],
        out_specs=[tab, tab],
        out_shape=[jax.ShapeDtypeStruct((m, dk), F32)] * 2,
        compiler_params=_params("arbitrary"),
        name="rope_tables",
    )(pos, inv_freq)


def _ret_in_kernel(h_ref, w_ref, cos_ref, sin_ref, o_ref, *, nq, nk, nv, head_qk):
    j = pl.program_id(1)
    y = jnp.dot(h_ref[...], w_ref[...], preferred_element_type=F32)
    bn = y.shape[1]

    def rope(scale):
        cos = cos_ref[...]
        sin = sin_ref[...]
        even = (lax.broadcasted_iota(jnp.int32, cos.shape, 1) & 1) == 0
        for hh in range(bn // head_qk):
            cols = slice(hh * head_qk, (hh + 1) * head_qk)
            t = y[:, cols]
            nxt = pltpu.roll(t, head_qk - 1, 1)
            prv = pltpu.roll(t, 1, 1)
            r = t * cos + jnp.where(even, -nxt, prv) * sin
            if scale is not None:
                r = r * scale
            o_ref[:, cols] = r.astype(o_ref.dtype)

    @pl.when(j < nq)
    def _():
        rope(None)

    @pl.when((j >= nq) & (j < nq + nk))
    def _():
        rope(head_qk ** -0.5)

    @pl.when((j >= nq + nk) & (j < nq + nk + nv))
    def _():
        o_ref[...] = y.astype(o_ref.dtype)

    @pl.when(j >= nq + nk + nv)
    def _():
        o_ref[...] = jax.nn.silu(y).astype(o_ref.dtype)


def _ret_in(h, w, cos, sin, *, bm, bn, qk_dim, v_dim, head_qk):
    m, d = h.shape
    n = w.shape[1]
    tab = pl.BlockSpec((bm, head_qk), lambda i, j: (i, 0))
    kern = functools.partial(_ret_in_kernel, nq=qk_dim // bn, nk=qk_dim // bn, nv=v_dim // bn,
                             head_qk=head_qk)
    return pl.pallas_call(
        kern,
        grid=(m // bm, n // bn),
        in_specs=[pl.BlockSpec((bm, d), lambda i, j: (i, 0)), pl.BlockSpec((d, bn), lambda i, j: (0, j)),
                  tab, tab],
        out_specs=pl.BlockSpec((bm, bn), lambda i, j: (i, j)),
        out_shape=jax.ShapeDtypeStruct((m, n), BF16),
        compiler_params=_params("arbitrary", "arbitrary"),
        name="ret_in",
    )(h, w, cos, sin)


def _ret_core_kernel(q_ref, k_ref, v_ref, g_ref, dm_ref, qd_ref, kd_ref, cd_ref, o_ref, state_ref):
    t = pl.program_id(2)

    @pl.when(t == 0)
    def _():
        state_ref[...] = jnp.zeros_like(state_ref)

    dmask = dm_ref[0]
    qdec = qd_ref[0]
    kdec = kd_ref[0]
    cdec = cd_ref[0]
    state = state_ref[...]
    for c in range(q_ref.shape[0] // CHUNK):
        rows = slice(c * CHUNK, (c + 1) * CHUNK)
        qc = q_ref[rows, :]
        kc = k_ref[rows, :]
        vc = v_ref[rows, :]
        scores = lax.dot_general(qc, kc, (((1,), (1,)), ((), ())), preferred_element_type=F32) * dmask
        intra = jnp.dot(scores.astype(BF16), vc, preferred_element_type=F32)
        qd = (qc.astype(F32) * qdec).astype(BF16)
        cross = jnp.dot(qd, state.astype(BF16), preferred_element_type=F32)
        kd = (kc.astype(F32) * kdec).astype(BF16)
        upd = lax.dot_general(kd, vc, (((0,), (0,)), ((), ())), preferred_element_type=F32)
        state = state * cdec + upd
        o = _rms(intra + cross)
        o_ref[rows, :] = (g_ref[rows, :].astype(F32) * o).astype(o_ref.dtype)
    state_ref[...] = state


def _ret_core(proj, dmask, qdec, kdec, cdec, *, batch, seq, ts, qk_dim, v_dim):
    m = proj.shape[0]
    hq = qk_dim // RET_HEADS
    hv = v_dim // RET_HEADS
    nts = seq // ts
    k_off = qk_dim // hq
    v_off = 2 * qk_dim // hv
    g_off = (2 * qk_dim + v_dim) // hv
    return pl.pallas_call(
        _ret_core_kernel,
        grid=(batch, RET_HEADS, nts),
        in_specs=[
            pl.BlockSpec((ts, hq), lambda b, h, t: (b * nts + t, h)),
            pl.BlockSpec((ts, hq), lambda b, h, t: (b * nts + t, k_off + h)),
            pl.BlockSpec((ts, hv), lambda b, h, t: (b * nts + t, v_off + h)),
            pl.BlockSpec((ts, hv), lambda b, h, t: (b * nts + t, g_off + h)),
            pl.BlockSpec((1, CHUNK, CHUNK), lambda b, h, t: (h, 0, 0)),
            pl.BlockSpec((1, CHUNK, 1), lambda b, h, t: (h, 0, 0)),
            pl.BlockSpec((1, CHUNK, 1), lambda b, h, t: (h, 0, 0)),
            pl.BlockSpec((1, 1, 1), lambda b, h, t: (h, 0, 0)),
        ],
        out_specs=pl.BlockSpec((ts, hv), lambda b, h, t: (b * nts + t, h)),
        out_shape=jax.ShapeDtypeStruct((m, v_dim), BF16),
        scratch_shapes=[pltpu.VMEM((hq, hv), F32)],
        compiler_params=_params("arbitrary", "arbitrary", "arbitrary"),
        name="ret_core",
    )(proj, proj, proj, proj, dmask, qdec, kdec, cdec)


def _retention_constants(head_qk):
    log_gamma = jnp.log1p(-jnp.exp2(-5.0 - jnp.arange(RET_HEADS, dtype=F32)))
    idx = jnp.arange(CHUNK, dtype=F32)
    rel = idx[:, None] - idx[None, :]
    dmask = jnp.where(rel[None] >= 0,
                      jnp.exp(jnp.maximum(rel, 0.0)[None] * log_gamma[:, None, None]), 0.0)
    qdec = jnp.exp((idx[None, :] + 1.0) * log_gamma[:, None])[:, :, None]
    kdec = jnp.exp((CHUNK - 1.0 - idx)[None, :] * log_gamma[:, None])[:, :, None]
    cdec = jnp.exp(CHUNK * log_gamma)[:, None, None]
    inv_freq = 1.0 / (ROPE_BASE ** jnp.linspace(0.0, 1.0, head_qk // 2, dtype=F32))
    inv_freq = jnp.repeat(inv_freq, 2)[None, :]
    return dmask, qdec, kdec, cdec, inv_freq


def kernel(x, positions, mix_pre_g, mix_post_g, gmlp_w_in, gmlp_ln_g, gmlp_ln_b, gmlp_w_s, gmlp_b_s,
           gmlp_w_out, ret_w_in, ret_w_out, ffn_pre_g, ffn_post_g, ffn_w_up, ffn_conv_w, ffn_conv_b,
           ffn_w_down):
    batch, seq, d = x.shape
    m = batch * seq
    qk_dim = d
    v_dim = (ret_w_in.shape[2] - 2 * qk_dim) // 2
    head_qk = qk_dim // RET_HEADS
    xf = x.reshape(m, d)

    h = _prenorm(xf, mix_pre_g[0], bm=512)
    z, mu, rstd = _gmlp_in(h, gmlp_w_in[0].astype(BF16), bm=1024, bn=1024)
    xf, h = _gmlp_down(z, mu, rstd, gmlp_ln_g[0], gmlp_ln_b[0], gmlp_w_s[0], gmlp_b_s[0],
                       gmlp_w_out[0].astype(BF16), xf, mix_post_g[0], ffn_pre_g[0],
                       bm=512, groups_per_step=2)
    a = _ffn_up(h, ffn_w_up[0].astype(BF16), ffn_conv_w[0], ffn_conv_b[0], seq=seq, bm=1024, tn=512)
    xf, h = _down_proj(a, ffn_w_down[0].astype(BF16), xf, ffn_post_g[0], mix_pre_g[1], bm=512, tk=1408)

    dmask, qdec, kdec, cdec, inv_freq = _retention_constants(head_qk)
    cos, sin = _rope_tables(positions.reshape(m, 1), inv_freq, bm=1024)
    proj = _ret_in(h, ret_w_in[0].astype(BF16), cos, sin, bm=1024, bn=1024,
                   qk_dim=qk_dim, v_dim=v_dim, head_qk=head_qk)
    o = _ret_core(proj, dmask, qdec, kdec, cdec, batch=batch, seq=seq, ts=512,
                  qk_dim=qk_dim, v_dim=v_dim)
    xf, h = _down_proj(o, ret_w_out[0].astype(BF16), xf, mix_post_g[1], ffn_pre_g[1], bm=512, tk=1024)
    a = _ffn_up(h, ffn_w_up[1].astype(BF16), ffn_conv_w[1], ffn_conv_b[1], seq=seq, bm=1024, tn=512)
    xf, _ = _down_proj(a, ffn_w_down[1].astype(BF16), xf, ffn_post_g[1], ffn_post_g[1], bm=512, tk=1408,
                       with_next=False)
    return xf.reshape(batch, seq, d)
```

```python
import functools

import jax
import jax.numpy as jnp
import numpy as np
from jax import lax
from jax.experimental import pallas as pl
from jax.experimental.pallas import tpu as pltpu

EPS = 1e-6
CHUNK = 128
GMLP_GROUPS = 8
RET_HEADS = 8
ROPE_BASE = 10000.0
CONV_WIDTH = 3

F32 = jnp.float32
BF16 = jnp.bfloat16

V7X_VMEM_BYTES = 64 * 1024 * 1024
VMEM_LIMIT_BYTES = V7X_VMEM_BYTES - 8 * 1024 * 1024
LANES = 128
SUBLANES = 8


def _params(*semantics):
    return pltpu.CompilerParams(dimension_semantics=semantics, vmem_limit_bytes=VMEM_LIMIT_BYTES)


def _rms(x):
    return x * lax.rsqrt(jnp.mean(x * x, axis=-1, keepdims=True) + EPS)


def _staged_pipeline(s, n, stages):
    depth = len(stages)

    def run(active, step_parity):
        for k in active:
            stages[k]((step_parity - k) % 2)

    for fill in range(depth - 1):
        pl.when(s == fill)(functools.partial(run, range(fill + 1), fill % 2))
    for parity in (0, 1):
        pl.when((s >= depth - 1) & (s < n) & (s % 2 == parity))(
            functools.partial(run, range(depth), parity))
    for drain in range(1, depth):
        step = n - 1 + drain
        pl.when(s == step)(functools.partial(run, range(drain, depth), step % 2))


def _prenorm_kernel(x_ref, g_ref, o_ref):
    o_ref[...] = (_rms(x_ref[...]) * g_ref[...]).astype(o_ref.dtype)


def _prenorm(x, g, *, bm):
    m, d = x.shape
    return pl.pallas_call(
        _prenorm_kernel,
        grid=(m // bm,),
        in_specs=[pl.BlockSpec((bm, d), lambda i: (i, 0)), pl.BlockSpec((1, d), lambda i: (0, 0))],
        out_specs=pl.BlockSpec((bm, d), lambda i: (i, 0)),
        out_shape=jax.ShapeDtypeStruct((m, d), BF16),
        compiler_params=_params("arbitrary"),
        name="prenorm",
    )(x, g.reshape(1, d))


def _up_kernel(*refs, n_w, n_extra, n_out, weight_stationary, epilogue):
    h_ref = refs[0]
    w_refs = refs[1:1 + n_w]
    pos = 1 + n_w
    extra_refs = refs[pos:pos + n_extra]
    pos += n_extra
    out_refs = refs[pos:pos + n_out]
    pos += n_out
    if weight_stationary:
        wbf_refs = refs[pos:pos + n_w]
        pos += n_w
    else:
        wbf_refs = w_refs
    epi_scratch = refs[pos:]

    if weight_stationary:
        j, i = pl.program_id(0), pl.program_id(1)

        @pl.when(i == 0)
        def _():
            for w_ref, wbf_ref in zip(w_refs, wbf_refs):
                wbf_ref[...] = w_ref[...].astype(BF16)
    else:
        i, j = pl.program_id(0), pl.program_id(1)

    if epi_scratch:
        @pl.when((i == 0) & (j == 0))
        def _():
            for ref in epi_scratch:
                ref[...] = jnp.zeros_like(ref)

    h = h_ref[...]
    ys = [jnp.dot(h, wbf_ref[...], preferred_element_type=F32) for wbf_ref in wbf_refs]
    epilogue(ys, extra_refs, out_refs, epi_scratch, i, j)


def _up_proj(h, weights, *, n_col_tiles, bm, bn, weight_stationary, epilogue, extras=(), outs,
             epi_scratch=(), name):
    m, d = h.shape
    n_i, n_j = m // bm, n_col_tiles
    if weight_stationary:
        grid = (n_j, n_i)

        def ij(fn):
            return lambda j, i: fn(i, j)
    else:
        grid = (n_i, n_j)

        def ij(fn):
            return fn

    in_specs = [pl.BlockSpec((bm, d), ij(lambda i, j: (i, 0)))]
    for _, c0 in weights:
        in_specs.append(pl.BlockSpec((d, bn), ij(lambda i, j, c0=c0: (0, j + c0))))
    for _, shape, fn in extras:
        in_specs.append(pl.BlockSpec(shape, ij(fn)))
    out_specs = [pl.BlockSpec(shape, ij(fn)) for _, shape, fn in outs]
    scratch = []
    if weight_stationary:
        scratch += [pltpu.VMEM((d, bn), BF16) for _ in weights]
    scratch += list(epi_scratch)
    kern = functools.partial(_up_kernel, n_w=len(weights), n_extra=len(extras), n_out=len(outs),
                             weight_stationary=weight_stationary, epilogue=epilogue)
    return pl.pallas_call(
        kern,
        grid=grid,
        in_specs=in_specs,
        out_specs=out_specs,
        out_shape=[sds for sds, _, _ in outs],
        scratch_shapes=scratch,
        compiler_params=_params("arbitrary", "arbitrary"),
        name=name,
    )(h, *[w for w, _ in weights], *[a for a, _, _ in extras])


def _lane_partial_sum(y):
    acc = y[:, 0:LANES]
    for c in range(1, y.shape[1] // LANES):
        acc = acc + y[:, c * LANES:(c + 1) * LANES]
    return acc


def _gelu_exact(y):
    return 0.5 * y * (1.0 + lax.erf(y * np.float32(np.sqrt(0.5))))


def _gmlp_in_epilogue(ys, extras, outs, scratch, ip, jp, *, n_u_tiles, n_tiles, v_width):
    z_ref, mu_ref, rstd_ref = outs
    s_ref, ss_ref = scratch
    z = _gelu_exact(ys[0])
    z_ref[...] = z.astype(z_ref.dtype)
    carry = jp > n_u_tiles
    in_v = jp >= n_u_tiles
    s_ref[...] = jnp.where(carry, s_ref[...], 0.0) + jnp.where(in_v, _lane_partial_sum(z), 0.0)
    ss_ref[...] = jnp.where(carry, ss_ref[...], 0.0) + jnp.where(in_v, _lane_partial_sum(z * z), 0.0)

    @pl.when(jp == n_tiles - 1)
    def _():
        mu = jnp.sum(s_ref[...], axis=-1, keepdims=True) / v_width
        var = jnp.sum(ss_ref[...], axis=-1, keepdims=True) / v_width - mu * mu
        mu_ref[...] = mu
        rstd_ref[...] = lax.rsqrt(var + EPS)


def _gmlp_in(h, w, *, bm, bn):
    m = h.shape[0]
    n = w.shape[1]
    n_tiles = n // bn
    epilogue = functools.partial(_gmlp_in_epilogue, n_u_tiles=n_tiles // 2, n_tiles=n_tiles,
                                 v_width=n // 2)
    return _up_proj(
        h, [(w, 0)], n_col_tiles=n_tiles, bm=bm, bn=bn, weight_stationary=False, epilogue=epilogue,
        outs=[(jax.ShapeDtypeStruct((m, n), BF16), (bm, bn), lambda i, j: (i, j)),
              (jax.ShapeDtypeStruct((m, 1), F32), (bm, 1), lambda i, j: (i, 0)),
              (jax.ShapeDtypeStruct((m, 1), F32), (bm, 1), lambda i, j: (i, 0))],
        epi_scratch=[pltpu.VMEM((bm, LANES), F32), pltpu.VMEM((bm, LANES), F32)],
        name="gmlp_in")


def _causal_conv(z, prev, cw, cb):
    row = lax.broadcasted_iota(jnp.int32, z.shape, 0)
    z1 = jnp.where(row == 0, prev[7:8, :], pltpu.roll(z, 1, 0))
    z2 = jnp.where(row == 0, prev[6:7, :], jnp.where(row == 1, prev[7:8, :], pltpu.roll(z, 2, 0)))
    return cb + cw[0:1, :] * z2 + cw[1:2, :] * z1 + cw[2:3, :] * z


def _ffn_up_epilogue(ys, extras, outs, scratch, ip, jp, *, tiles_per_seq):
    cwg_ref, cwu_ref, cbg_ref, cbu_ref = extras
    seq_start = ip % tiles_per_seq == 0

    def branch(z, cw_ref, cb_ref, carry_ref):
        prev = jnp.where(seq_start, 0.0, carry_ref[...])
        out = _causal_conv(z, prev, cw_ref[...], cb_ref[...])
        carry_ref[...] = z[z.shape[0] - SUBLANES:, :]
        return out

    gate = branch(ys[0], cwg_ref, cbg_ref, scratch[0])
    up = branch(ys[1], cwu_ref, cbu_ref, scratch[1])
    outs[0][...] = (jax.nn.silu(gate) * up).astype(outs[0].dtype)


def _ffn_up(h, w, conv_w, conv_b, *, seq, bm, tn):
    m = h.shape[0]
    f = w.shape[1] // 2
    nj = f // tn
    cb = conv_b.reshape(1, 2 * f)
    return _up_proj(
        h, [(w, 0), (w, nj)], n_col_tiles=nj, bm=bm, bn=tn, weight_stationary=True,
        epilogue=functools.partial(_ffn_up_epilogue, tiles_per_seq=seq // bm),
        extras=[(conv_w, (CONV_WIDTH, tn), lambda i, j: (0, j)),
                (conv_w, (CONV_WIDTH, tn), lambda i, j: (0, j + nj)),
                (cb, (1, tn), lambda i, j: (0, j)),
                (cb, (1, tn), lambda i, j: (0, j + nj))],
        outs=[(jax.ShapeDtypeStruct((m, f), BF16), (bm, tn), lambda i, j: (i, j))],
        epi_scratch=[pltpu.VMEM((SUBLANES, tn), F32), pltpu.VMEM((SUBLANES, tn), F32)],
        name="ffn_up")[0]


def _rope_epilogue(ys, extras, outs, scratch, ip, jp, *, n_q_tiles, head_qk):
    cos = extras[0][...]
    sin = extras[1][...]
    y, o_ref = ys[0], outs[0]
    even = (lax.broadcasted_iota(jnp.int32, cos.shape, 1) & 1) == 0
    scale = jnp.where(jp >= n_q_tiles, head_qk ** -0.5, 1.0).astype(F32)
    for hh in range(y.shape[1] // head_qk):
        cols = slice(hh * head_qk, (hh + 1) * head_qk)
        t = y[:, cols]
        nxt = pltpu.roll(t, head_qk - 1, 1)
        prv = pltpu.roll(t, 1, 1)
        r = (t * cos + jnp.where(even, -nxt, prv) * sin) * scale
        o_ref[:, cols] = r.astype(o_ref.dtype)


def _value_gate_epilogue(ys, extras, outs, scratch, ip, jp, *, n_v_tiles):
    y = ys[0]
    outs[0][...] = jnp.where(jp >= n_v_tiles, jax.nn.silu(y), y).astype(outs[0].dtype)


def _ret_in(h, w, cos, sin, *, bm, bn, qk_dim, v_dim, head_qk):
    m = h.shape[0]
    nq = qk_dim // bn
    nv = v_dim // bn
    tab = (bm, head_qk)
    qk = _up_proj(
        h, [(w, 0)], n_col_tiles=2 * nq, bm=bm, bn=bn, weight_stationary=True,
        epilogue=functools.partial(_rope_epilogue, n_q_tiles=nq, head_qk=head_qk),
        extras=[(cos, tab, lambda i, j: (i, 0)), (sin, tab, lambda i, j: (i, 0))],
        outs=[(jax.ShapeDtypeStruct((m, 2 * qk_dim), BF16), (bm, bn), lambda i, j: (i, j))],
        name="ret_in_qk")[0]
    vg = _up_proj(
        h, [(w, 2 * nq)], n_col_tiles=2 * nv, bm=bm, bn=bn, weight_stationary=True,
        epilogue=functools.partial(_value_gate_epilogue, n_v_tiles=nv),
        outs=[(jax.ShapeDtypeStruct((m, 2 * v_dim), BF16), (bm, bn), lambda i, j: (i, j))],
        name="ret_in_vg")[0]
    return qk, vg


def _residual_norm_epilogue(acc, x_ref, pg_ref, ng_ref, xo_ref, ho_ref):
    xn = x_ref[...] + _rms(acc) * pg_ref[...]
    xo_ref[...] = xn
    if ho_ref is not None:
        ho_ref[...] = (_rms(xn) * ng_ref[...]).astype(ho_ref.dtype)


def _down_kernel(a_ref, w_ref, x_ref, pg_ref, ng_ref, *rest, with_next):
    if with_next:
        xo_ref, ho_ref, raw0_ref, raw1_ref = rest
    else:
        (xo_ref, raw0_ref, raw1_ref), ho_ref = rest, None
    raws = (raw0_ref, raw1_ref)

    def matmul(par):
        raws[par][...] = jnp.dot(a_ref[...], w_ref[...], preferred_element_type=F32)

    def finish(par):
        _residual_norm_epilogue(raws[par][...], x_ref, pg_ref, ng_ref, xo_ref, ho_ref)

    _staged_pipeline(pl.program_id(0), pl.num_programs(0) - 1, [matmul, finish])


def _down_proj(a, w, x, post_g, next_g, *, bm, with_next=True):
    m, kdim = a.shape
    d = w.shape[1]
    n = m // bm
    prev_row = pl.BlockSpec((bm, d), lambda s: (jnp.maximum(s - 1, 0), 0))
    gain = pl.BlockSpec((1, d), lambda s: (0, 0))
    out_specs = [prev_row, prev_row] if with_next else [prev_row]
    out_shape = [jax.ShapeDtypeStruct((m, d), F32)]
    if with_next:
        out_shape.append(jax.ShapeDtypeStruct((m, d), BF16))
    res = pl.pallas_call(
        functools.partial(_down_kernel, with_next=with_next),
        grid=(n + 1,),
        in_specs=[
            pl.BlockSpec((bm, kdim), lambda s: (jnp.minimum(s, n - 1), 0)),
            pl.BlockSpec((kdim, d), lambda s: (0, 0)),
            prev_row, gain, gain,
        ],
        out_specs=out_specs,
        out_shape=out_shape,
        scratch_shapes=[pltpu.VMEM((bm, d), F32), pltpu.VMEM((bm, d), F32)],
        compiler_params=_params("arbitrary"),
        name="down_proj",
    )(a, w, x, post_g.reshape(1, d), next_g.reshape(1, d))
    return res if with_next else (res[0], None)


def _gmlp_down_kernel(u_ref, v_ref, mu_ref, rstd_ref, lg_ref, lb_ref, ws_ref, bs_ref, w_ref,
                      x_ref, pg_ref, ng_ref, xo_ref, ho_ref, lhs0_ref, lhs1_ref, raw0_ref, raw1_ref):
    lhs = (lhs0_ref, lhs1_ref)
    raws = (raw0_ref, raw1_ref)
    bm, half = u_ref.shape
    gd = half // GMLP_GROUPS

    def gating(par):
        tril = (lax.broadcasted_iota(jnp.int32, (CHUNK, CHUNK), 0)
                >= lax.broadcasted_iota(jnp.int32, (CHUNK, CHUNK), 1))
        mu = mu_ref[...]
        rstd = rstd_ref[...]
        for g in range(GMLP_GROUPS):
            cols = slice(g * gd, (g + 1) * gd)
            ws = jnp.where(tril, ws_ref[g], 0.0).astype(BF16)
            bias = bs_ref[g]
            vn = (v_ref[:, cols].astype(F32) - mu) * rstd * lg_ref[:, cols] + lb_ref[:, cols]
            vn = vn.astype(BF16)
            for c in range(bm // CHUNK):
                rows = slice(c * CHUNK, (c + 1) * CHUNK)
                mixed = jnp.dot(ws, vn[rows], preferred_element_type=F32) + bias
                lhs[par][rows, cols] = (u_ref[rows, cols].astype(F32) * mixed).astype(BF16)

    def matmul(par):
        raws[par][...] = jnp.dot(lhs[par][...], w_ref[...], preferred_element_type=F32)

    def finish(par):
        _residual_norm_epilogue(raws[par][...], x_ref, pg_ref, ng_ref, xo_ref, ho_ref)

    _staged_pipeline(pl.program_id(0), pl.num_programs(0) - 2, [gating, matmul, finish])


def _gmlp_down(z, mu, rstd, ln_g, ln_b, w_s, b_s, w, x, post_g, next_g, *, bm):
    m = z.shape[0]
    half, d = w.shape
    n = m // bm

    def first(s):
        return jnp.minimum(s, n - 1)

    def last(s):
        return jnp.clip(s - 2, 0, n - 1)

    row = pl.BlockSpec((bm, d), lambda s: (last(s), 0))
    gain = pl.BlockSpec((1, d), lambda s: (0, 0))
    stat = pl.BlockSpec((bm, 1), lambda s: (first(s), 0))
    lnp = pl.BlockSpec((1, half), lambda s: (0, 0))
    return pl.pallas_call(
        _gmlp_down_kernel,
        grid=(n + 2,),
        in_specs=[
            pl.BlockSpec((bm, half), lambda s: (first(s), 0)),
            pl.BlockSpec((bm, half), lambda s: (first(s), 1)),
            stat, stat, lnp, lnp,
            pl.BlockSpec((GMLP_GROUPS, CHUNK, CHUNK), lambda s: (0, 0, 0)),
            pl.BlockSpec((GMLP_GROUPS, CHUNK, 1), lambda s: (0, 0, 0)),
            pl.BlockSpec((half, d), lambda s: (0, 0)),
            row, gain, gain,
        ],
        out_specs=[row, row],
        out_shape=[jax.ShapeDtypeStruct((m, d), F32), jax.ShapeDtypeStruct((m, d), BF16)],
        scratch_shapes=[pltpu.VMEM((bm, half), BF16), pltpu.VMEM((bm, half), BF16),
                        pltpu.VMEM((bm, d), F32), pltpu.VMEM((bm, d), F32)],
        compiler_params=_params("arbitrary"),
        name="gmlp_down",
    )(z, z, mu, rstd, ln_g.reshape(1, half), ln_b.reshape(1, half), w_s,
      b_s.reshape(GMLP_GROUPS, CHUNK, 1), w, x, post_g.reshape(1, d), next_g.reshape(1, d))


def _rope_table_kernel(pos_ref, inv_ref, cos_ref, sin_ref):
    ang = pos_ref[...].astype(F32) * inv_ref[...]
    cos_ref[...] = jnp.cos(ang)
    sin_ref[...] = jnp.sin(ang)


def _rope_tables(pos, inv_freq, *, bm):
    m = pos.shape[0]
    dk = inv_freq.shape[1]
    tab = pl.BlockSpec((bm, dk), lambda i: (i, 0))
    return pl.pallas_call(
        _rope_table_kernel,
        grid=(m // bm,),
        in_specs=[pl.BlockSpec((bm, 1), lambda i: (i, 0)), pl.BlockSpec((1, dk), lambda i: (0, 0))],
        out_specs=[tab, tab],
        out_shape=[jax.ShapeDtypeStruct((m, dk), F32)] * 2,
        compiler_params=_params("arbitrary"),
        name="rope_tables",
    )(pos, inv_freq)


def _ret_core_kernel(q_ref, k_ref, v_ref, g_ref, dm_ref, qd_ref, kd_ref, cd_ref, o_ref, state_ref):
    t = pl.program_id(2)

    @pl.when(t == 0)
    def _():
        state_ref[...] = jnp.zeros_like(state_ref)

    dmask = dm_ref[0]
    qdec = qd_ref[0]
    kdec = kd_ref[0]
    cdec = cd_ref[0]
    state = state_ref[...]
    for c in range(q_ref.shape[0] // CHUNK):
        rows = slice(c * CHUNK, (c + 1) * CHUNK)
        qc = q_ref[rows, :]
        kc = k_ref[rows, :]
        vc = v_ref[rows, :]
        scores = lax.dot_general(qc, kc, (((1,), (1,)), ((), ())), preferred_element_type=F32) * dmask
        intra = jnp.dot(scores.astype(BF16), vc, preferred_element_type=F32)
        qd = (qc.astype(F32) * qdec).astype(BF16)
        cross = jnp.dot(qd, state.astype(BF16), preferred_element_type=F32)
        kd = (kc.astype(F32) * kdec).astype(BF16)
        upd = lax.dot_general(kd, vc, (((0,), (0,)), ((), ())), preferred_element_type=F32)
        state = state * cdec + upd
        o = _rms(intra + cross)
        o_ref[rows, :] = (g_ref[rows, :].astype(F32) * o).astype(o_ref.dtype)
    state_ref[...] = state


def _ret_core(qk, vg, dmask, qdec, kdec, cdec, *, batch, seq, ts):
    m = qk.shape[0]
    hq = qk.shape[1] // (2 * RET_HEADS)
    hv = vg.shape[1] // (2 * RET_HEADS)
    nts = seq // ts
    return pl.pallas_call(
        _ret_core_kernel,
        grid=(batch, RET_HEADS, nts),
        in_specs=[
            pl.BlockSpec((ts, hq), lambda b, h, t: (b * nts + t, h)),
            pl.BlockSpec((ts, hq), lambda b, h, t: (b * nts + t, RET_HEADS + h)),
            pl.BlockSpec((ts, hv), lambda b, h, t: (b * nts + t, h)),
            pl.BlockSpec((ts, hv), lambda b, h, t: (b * nts + t, RET_HEADS + h)),
            pl.BlockSpec((1, CHUNK, CHUNK), lambda b, h, t: (h, 0, 0)),
            pl.BlockSpec((1, CHUNK, 1), lambda b, h, t: (h, 0, 0)),
            pl.BlockSpec((1, CHUNK, 1), lambda b, h, t: (h, 0, 0)),
            pl.BlockSpec((1, 1, 1), lambda b, h, t: (h, 0, 0)),
        ],
        out_specs=pl.BlockSpec((ts, hv), lambda b, h, t: (b * nts + t, h)),
        out_shape=jax.ShapeDtypeStruct((m, hv * RET_HEADS), BF16),
        scratch_shapes=[pltpu.VMEM((hq, hv), F32)],
        compiler_params=_params("arbitrary", "arbitrary", "arbitrary"),
        name="ret_core",
    )(qk, qk, vg, vg, dmask, qdec, kdec, cdec)


def _retention_constants(head_qk):
    log_gamma = jnp.log1p(-jnp.exp2(-5.0 - jnp.arange(RET_HEADS, dtype=F32)))
    idx = jnp.arange(CHUNK, dtype=F32)
    rel = idx[:, None] - idx[None, :]
    dmask = jnp.where(rel[None] >= 0,
                      jnp.exp(jnp.maximum(rel, 0.0)[None] * log_gamma[:, None, None]), 0.0)
    qdec = jnp.exp((idx[None, :] + 1.0) * log_gamma[:, None])[:, :, None]
    kdec = jnp.exp((CHUNK - 1.0 - idx)[None, :] * log_gamma[:, None])[:, :, None]
    cdec = jnp.exp(CHUNK * log_gamma)[:, None, None]
    inv_freq = 1.0 / (ROPE_BASE ** jnp.linspace(0.0, 1.0, head_qk // 2, dtype=F32))
    inv_freq = jnp.repeat(inv_freq, 2)[None, :]
    return dmask, qdec, kdec, cdec, inv_freq


def kernel(x, positions, mix_pre_g, mix_post_g, gmlp_w_in, gmlp_ln_g, gmlp_ln_b, gmlp_w_s, gmlp_b_s,
           gmlp_w_out, ret_w_in, ret_w_out, ffn_pre_g, ffn_post_g, ffn_w_up, ffn_conv_w, ffn_conv_b,
           ffn_w_down):
    batch, seq, d = x.shape
    m = batch * seq
    qk_dim = d
    v_dim = (ret_w_in.shape[2] - 2 * qk_dim) // 2
    head_qk = qk_dim // RET_HEADS
    xf = x.reshape(m, d)

    h = _prenorm(xf, mix_pre_g[0], bm=512)
    z, mu, rstd = _gmlp_in(h, gmlp_w_in[0].astype(BF16), bm=1024, bn=2048)
    xf, h = _gmlp_down(z, mu, rstd, gmlp_ln_g[0], gmlp_ln_b[0], gmlp_w_s[0], gmlp_b_s[0],
                       gmlp_w_out[0].astype(BF16), xf, mix_post_g[0], ffn_pre_g[0], bm=128)
    a = _ffn_up(h, ffn_w_up[0], ffn_conv_w[0], ffn_conv_b[0], seq=seq, bm=1024, tn=512)
    xf, h = _down_proj(a, ffn_w_down[0].astype(BF16), xf, ffn_post_g[0], mix_pre_g[1], bm=256)

    dmask, qdec, kdec, cdec, inv_freq = _retention_constants(head_qk)
    cos, sin = _rope_tables(positions.reshape(m, 1), inv_freq, bm=1024)
    qk, vg = _ret_in(h, ret_w_in[0], cos, sin, bm=1024, bn=1024,
                     qk_dim=qk_dim, v_dim=v_dim, head_qk=head_qk)
    o = _ret_core(qk, vg, dmask, qdec, kdec, cdec, batch=batch, seq=seq, ts=512)
    xf, h = _down_proj(o, ret_w_out[0].astype(BF16), xf, mix_post_g[1], ffn_pre_g[1], bm=256)
    a = _ffn_up(h, ffn_w_up[1], ffn_conv_w[1], ffn_conv_b[1], seq=seq, bm=1024, tn=512)
    xf, _ = _down_proj(a, ffn_w_down[1].astype(BF16), xf, ffn_post_g[1], ffn_post_g[1], bm=256,
                       with_next=False)
    return xf.reshape(batch, seq, d)
```

```python
import functools

import jax
import jax.numpy as jnp
import numpy as np
from jax import lax
from jax.experimental import pallas as pl
from jax.experimental.pallas import tpu as pltpu

EPS = 1e-6
CHUNK = 128
GMLP_GROUPS = 8
RET_HEADS = 8
ROPE_BASE = 10000.0
CONV_WIDTH = 3

F32 = jnp.float32
BF16 = jnp.bfloat16

V7X_VMEM_BYTES = 64 * 1024 * 1024
VMEM_LIMIT_BYTES = V7X_VMEM_BYTES - 8 * 1024 * 1024
LANES = 128
SUBLANES = 8


def _params(*semantics):
    return pltpu.CompilerParams(dimension_semantics=semantics, vmem_limit_bytes=VMEM_LIMIT_BYTES)


def _rms(x):
    return x * lax.rsqrt(jnp.mean(x * x, axis=-1, keepdims=True) + EPS)


def _staged_pipeline(s, n, stages):
    depth = len(stages)

    def run(active, step_parity):
        for k in active:
            stages[k]((step_parity - k) % 2)

    for fill in range(depth - 1):
        pl.when(s == fill)(functools.partial(run, range(fill + 1), fill % 2))
    for parity in (0, 1):
        pl.when((s >= depth - 1) & (s < n) & (s % 2 == parity))(
            functools.partial(run, range(depth), parity))
    for drain in range(1, depth):
        step = n - 1 + drain
        pl.when(s == step)(functools.partial(run, range(drain, depth), step % 2))


def _prenorm_kernel(x_ref, g_ref, o_ref):
    o_ref[...] = (_rms(x_ref[...]) * g_ref[...]).astype(o_ref.dtype)


def _prenorm(x, g, *, bm):
    m, d = x.shape
    return pl.pallas_call(
        _prenorm_kernel,
        grid=(m // bm,),
        in_specs=[pl.BlockSpec((bm, d), lambda i: (i, 0)), pl.BlockSpec((1, d), lambda i: (0, 0))],
        out_specs=pl.BlockSpec((bm, d), lambda i: (i, 0)),
        out_shape=jax.ShapeDtypeStruct((m, d), BF16),
        compiler_params=_params("arbitrary"),
        name="prenorm",
    )(x, g.reshape(1, d))


def _up_kernel(*refs, n_w, n_extra, n_out, weight_stationary, epilogue):
    h_ref = refs[0]
    w_refs = refs[1:1 + n_w]
    pos = 1 + n_w
    extra_refs = refs[pos:pos + n_extra]
    pos += n_extra
    out_refs = refs[pos:pos + n_out]
    pos += n_out
    if weight_stationary:
        wbf_refs = refs[pos:pos + n_w]
        pos += n_w
    else:
        wbf_refs = w_refs
    epi_scratch = refs[pos:]

    if weight_stationary:
        j, i = pl.program_id(0), pl.program_id(1)

        @pl.when(i == 0)
        def _():
            for w_ref, wbf_ref in zip(w_refs, wbf_refs):
                wbf_ref[...] = w_ref[...].astype(BF16)
    else:
        i, j = pl.program_id(0), pl.program_id(1)

    if epi_scratch:
        @pl.when((i == 0) & (j == 0))
        def _():
            for ref in epi_scratch:
                ref[...] = jnp.zeros_like(ref)

    h = h_ref[...]
    ys = [jnp.dot(h, wbf_ref[...], preferred_element_type=F32) for wbf_ref in wbf_refs]
    epilogue(ys, extra_refs, out_refs, epi_scratch, i, j)


def _up_proj(h, weights, *, n_col_tiles, bm, bn, weight_stationary, epilogue, extras=(), outs,
             epi_scratch=(), name):
    m, d = h.shape
    n_i, n_j = m // bm, n_col_tiles
    if weight_stationary:
        grid = (n_j, n_i)

        def ij(fn):
            return lambda j, i: fn(i, j)
    else:
        grid = (n_i, n_j)

        def ij(fn):
            return fn

    in_specs = [pl.BlockSpec((bm, d), ij(lambda i, j: (i, 0)))]
    for _, layer, c0 in weights:
        in_specs.append(pl.BlockSpec((None, d, bn), ij(lambda i, j, layer=layer, c0=c0: (layer, 0, j + c0))))
    for _, shape, fn in extras:
        in_specs.append(pl.BlockSpec(shape, ij(fn)))
    out_specs = [pl.BlockSpec(shape, ij(fn)) for _, shape, fn in outs]
    scratch = []
    if weight_stationary:
        scratch += [pltpu.VMEM((d, bn), BF16) for _ in weights]
    scratch += list(epi_scratch)
    kern = functools.partial(_up_kernel, n_w=len(weights), n_extra=len(extras), n_out=len(outs),
                             weight_stationary=weight_stationary, epilogue=epilogue)
    return pl.pallas_call(
        kern,
        grid=grid,
        in_specs=in_specs,
        out_specs=out_specs,
        out_shape=[sds for sds, _, _ in outs],
        scratch_shapes=scratch,
        compiler_params=_params("arbitrary", "arbitrary"),
        name=name,
    )(h, *[w for w, _, _ in weights], *[a for a, _, _ in extras])


def _lane_partial_sum(y):
    acc = y[:, 0:LANES]
    for c in range(1, y.shape[1] // LANES):
        acc = acc + y[:, c * LANES:(c + 1) * LANES]
    return acc


def _gelu_exact(y):
    return 0.5 * y * (1.0 + lax.erf(y * np.float32(np.sqrt(0.5))))


def _gmlp_in_epilogue(ys, extras, outs, scratch, ip, jp, *, n_u_tiles, n_tiles, v_width):
    z_ref, mu_ref, rstd_ref = outs
    s_ref, ss_ref = scratch
    z = _gelu_exact(ys[0])
    z_ref[...] = z.astype(z_ref.dtype)
    carry = jp > n_u_tiles
    in_v = jp >= n_u_tiles
    s_ref[...] = jnp.where(carry, s_ref[...], 0.0) + jnp.where(in_v, _lane_partial_sum(z), 0.0)
    ss_ref[...] = jnp.where(carry, ss_ref[...], 0.0) + jnp.where(in_v, _lane_partial_sum(z * z), 0.0)

    @pl.when(jp == n_tiles - 1)
    def _():
        mu = jnp.sum(s_ref[...], axis=-1, keepdims=True) / v_width
        var = jnp.sum(ss_ref[...], axis=-1, keepdims=True) / v_width - mu * mu
        mu_ref[...] = mu
        rstd_ref[...] = lax.rsqrt(var + EPS)


def _gmlp_in(h, w, *, bm, bn):
    m = h.shape[0]
    n = w.shape[2]
    n_tiles = n // bn
    epilogue = functools.partial(_gmlp_in_epilogue, n_u_tiles=n_tiles // 2, n_tiles=n_tiles,
                                 v_width=n // 2)
    return _up_proj(
        h, [(w, 0, 0)], n_col_tiles=n_tiles, bm=bm, bn=bn, weight_stationary=False, epilogue=epilogue,
        outs=[(jax.ShapeDtypeStruct((m, n), BF16), (bm, bn), lambda i, j: (i, j)),
              (jax.ShapeDtypeStruct((m, 1), F32), (bm, 1), lambda i, j: (i, 0)),
              (jax.ShapeDtypeStruct((m, 1), F32), (bm, 1), lambda i, j: (i, 0))],
        epi_scratch=[pltpu.VMEM((bm, LANES), F32), pltpu.VMEM((bm, LANES), F32)],
        name="gmlp_in")


def _causal_conv(z, prev, cw, cb):
    row = lax.broadcasted_iota(jnp.int32, z.shape, 0)
    z1 = jnp.where(row == 0, prev[7:8, :], pltpu.roll(z, 1, 0))
    z2 = jnp.where(row == 0, prev[6:7, :], jnp.where(row == 1, prev[7:8, :], pltpu.roll(z, 2, 0)))
    return cb + cw[0:1, :] * z2 + cw[1:2, :] * z1 + cw[2:3, :] * z


def _ffn_up_epilogue(ys, extras, outs, scratch, ip, jp, *, tiles_per_seq):
    cwg_ref, cwu_ref, cbg_ref, cbu_ref = extras
    seq_start = ip % tiles_per_seq == 0

    def branch(z, cw_ref, cb_ref, carry_ref):
        prev = jnp.where(seq_start, 0.0, carry_ref[...])
        out = _causal_conv(z, prev, cw_ref[...], cb_ref[...])
        carry_ref[...] = z[z.shape[0] - SUBLANES:, :]
        return out

    gate = branch(ys[0], cwg_ref, cbg_ref, scratch[0])
    up = branch(ys[1], cwu_ref, cbu_ref, scratch[1])
    outs[0][...] = (jax.nn.silu(gate) * up).astype(outs[0].dtype)


def _ffn_up(h, w, conv_w, conv_b, layer, *, seq, bm, tn):
    m = h.shape[0]
    f = w.shape[2] // 2
    nj = f // tn
    cb = conv_b.reshape(conv_b.shape[0], 1, 2 * f)
    return _up_proj(
        h, [(w, layer, 0), (w, layer, nj)], n_col_tiles=nj, bm=bm, bn=tn, weight_stationary=True,
        epilogue=functools.partial(_ffn_up_epilogue, tiles_per_seq=seq // bm),
        extras=[(conv_w, (None, CONV_WIDTH, tn), lambda i, j: (layer, 0, j)),
                (conv_w, (None, CONV_WIDTH, tn), lambda i, j: (layer, 0, j + nj)),
                (cb, (None, 1, tn), lambda i, j: (layer, 0, j)),
                (cb, (None, 1, tn), lambda i, j: (layer, 0, j + nj))],
        outs=[(jax.ShapeDtypeStruct((m, f), BF16), (bm, tn), lambda i, j: (i, j))],
        epi_scratch=[pltpu.VMEM((SUBLANES, tn), F32), pltpu.VMEM((SUBLANES, tn), F32)],
        name="ffn_up")[0]


def _rope_epilogue(ys, extras, outs, scratch, ip, jp, *, n_q_tiles, head_qk):
    cos = extras[0][...]
    sin = extras[1][...]
    y, o_ref = ys[0], outs[0]
    even = (lax.broadcasted_iota(jnp.int32, cos.shape, 1) & 1) == 0
    scale = jnp.where(jp >= n_q_tiles, head_qk ** -0.5, 1.0).astype(F32)
    for hh in range(y.shape[1] // head_qk):
        cols = slice(hh * head_qk, (hh + 1) * head_qk)
        t = y[:, cols]
        nxt = pltpu.roll(t, head_qk - 1, 1)
        prv = pltpu.roll(t, 1, 1)
        r = (t * cos + jnp.where(even, -nxt, prv) * sin) * scale
        o_ref[:, cols] = r.astype(o_ref.dtype)


def _value_gate_epilogue(ys, extras, outs, scratch, ip, jp, *, n_v_tiles):
    y = ys[0]
    outs[0][...] = jnp.where(jp >= n_v_tiles, jax.nn.silu(y), y).astype(outs[0].dtype)


def _ret_in(h, w, cos, sin, *, bm, bn, qk_dim, v_dim, head_qk):
    m = h.shape[0]
    nq = qk_dim // bn
    nv = v_dim // bn
    tab = (bm, head_qk)
    qk = _up_proj(
        h, [(w, 0, 0)], n_col_tiles=2 * nq, bm=bm, bn=bn, weight_stationary=True,
        epilogue=functools.partial(_rope_epilogue, n_q_tiles=nq, head_qk=head_qk),
        extras=[(cos, tab, lambda i, j: (i, 0)), (sin, tab, lambda i, j: (i, 0))],
        outs=[(jax.ShapeDtypeStruct((m, 2 * qk_dim), BF16), (bm, bn), lambda i, j: (i, j))],
        name="ret_in_qk")[0]
    vg = _up_proj(
        h, [(w, 0, 2 * nq)], n_col_tiles=2 * nv, bm=bm, bn=bn, weight_stationary=True,
        epilogue=functools.partial(_value_gate_epilogue, n_v_tiles=nv),
        outs=[(jax.ShapeDtypeStruct((m, 2 * v_dim), BF16), (bm, bn), lambda i, j: (i, j))],
        name="ret_in_vg")[0]
    return qk, vg


def _residual_norm_epilogue(acc, x_ref, pg_ref, ng_ref, xo_ref, ho_ref):
    xn = x_ref[...] + _rms(acc) * pg_ref[...]
    xo_ref[...] = xn
    if ho_ref is not None:
        ho_ref[...] = (_rms(xn) * ng_ref[...]).astype(ho_ref.dtype)


def _down_kernel(a_ref, w_ref, x_ref, pg_ref, *rest, with_next):
    if with_next:
        ng_ref, xo_ref, ho_ref, raw0_ref, raw1_ref = rest
    else:
        (xo_ref, raw0_ref, raw1_ref), ng_ref, ho_ref = rest, None, None
    raws = (raw0_ref, raw1_ref)

    def matmul(par):
        raws[par][...] = jnp.dot(a_ref[...], w_ref[...], preferred_element_type=F32)

    def finish(par):
        _residual_norm_epilogue(raws[par][...], x_ref, pg_ref, ng_ref, xo_ref, ho_ref)

    _staged_pipeline(pl.program_id(0), pl.num_programs(0) - 1, [matmul, finish])


def _down_proj(a, w, x, post_g, next_g=None, *, bm):
    m, kdim = a.shape
    d = w.shape[1]
    n = m // bm
    with_next = next_g is not None
    prev_row = pl.BlockSpec((bm, d), lambda s: (jnp.maximum(s - 1, 0), 0))
    gain = pl.BlockSpec((1, d), lambda s: (0, 0))
    gains = [post_g.reshape(1, d)] + ([next_g.reshape(1, d)] if with_next else [])
    out_shape = [jax.ShapeDtypeStruct((m, d), F32)]
    if with_next:
        out_shape.append(jax.ShapeDtypeStruct((m, d), BF16))
    res = pl.pallas_call(
        functools.partial(_down_kernel, with_next=with_next),
        grid=(n + 1,),
        in_specs=[
            pl.BlockSpec((bm, kdim), lambda s: (jnp.minimum(s, n - 1), 0)),
            pl.BlockSpec((kdim, d), lambda s: (0, 0)),
            prev_row,
        ] + [gain] * len(gains),
        out_specs=[prev_row] * len(out_shape),
        out_shape=out_shape,
        scratch_shapes=[pltpu.VMEM((bm, d), F32), pltpu.VMEM((bm, d), F32)],
        compiler_params=_params("arbitrary"),
        name="down_proj",
    )(a, w, x, *gains)
    return res if with_next else (res[0], None)


def _gmlp_down_kernel(u_ref, v_ref, mu_ref, rstd_ref, lg_ref, lb_ref, ws_ref, bs_ref, w_ref,
                      x_ref, pg_ref, ng_ref, xo_ref, ho_ref, raw0_ref, raw1_ref):
    raws = (raw0_ref, raw1_ref)
    bm, half = u_ref.shape
    gd = half // GMLP_GROUPS

    def gate_and_project(par):
        tril = (lax.broadcasted_iota(jnp.int32, (CHUNK, CHUNK), 0)
                >= lax.broadcasted_iota(jnp.int32, (CHUNK, CHUNK), 1))
        mu = mu_ref[...]
        rstd = rstd_ref[...]
        acc = None
        for g in range(GMLP_GROUPS):
            cols = slice(g * gd, (g + 1) * gd)
            ws = jnp.where(tril, ws_ref[g], 0.0).astype(BF16)
            bias = bs_ref[g]
            vn = (v_ref[:, cols].astype(F32) - mu) * rstd * lg_ref[:, cols] + lb_ref[:, cols]
            vn = vn.astype(BF16)
            gated = []
            for c in range(bm // CHUNK):
                rows = slice(c * CHUNK, (c + 1) * CHUNK)
                mixed = jnp.dot(ws, vn[rows], preferred_element_type=F32) + bias
                gated.append((u_ref[rows, cols].astype(F32) * mixed).astype(BF16))
            part = jnp.dot(jnp.concatenate(gated, axis=0), w_ref[cols, :], preferred_element_type=F32)
            acc = part if acc is None else acc + part
        raws[par][...] = acc

    def finish(par):
        _residual_norm_epilogue(raws[par][...], x_ref, pg_ref, ng_ref, xo_ref, ho_ref)

    _staged_pipeline(pl.program_id(0), pl.num_programs(0) - 1, [gate_and_project, finish])


def _gmlp_down(z, mu, rstd, ln_g, ln_b, w_s, b_s, w, x, post_g, next_g, *, bm):
    m = z.shape[0]
    half, d = w.shape
    n = m // bm

    def first(s):
        return jnp.minimum(s, n - 1)

    def last(s):
        return jnp.maximum(s - 1, 0)

    row = pl.BlockSpec((bm, d), lambda s: (last(s), 0))
    gain = pl.BlockSpec((1, d), lambda s: (0, 0))
    stat = pl.BlockSpec((bm, 1), lambda s: (first(s), 0))
    lnp = pl.BlockSpec((1, half), lambda s: (0, 0))
    return pl.pallas_call(
        _gmlp_down_kernel,
        grid=(n + 1,),
        in_specs=[
            pl.BlockSpec((bm, half), lambda s: (first(s), 0)),
            pl.BlockSpec((bm, half), lambda s: (first(s), 1)),
            stat, stat, lnp, lnp,
            pl.BlockSpec((GMLP_GROUPS, CHUNK, CHUNK), lambda s: (0, 0, 0)),
            pl.BlockSpec((GMLP_GROUPS, CHUNK, 1), lambda s: (0, 0, 0)),
            pl.BlockSpec((half, d), lambda s: (0, 0)),
            row, gain, gain,
        ],
        out_specs=[row, row],
        out_shape=[jax.ShapeDtypeStruct((m, d), F32), jax.ShapeDtypeStruct((m, d), BF16)],
        scratch_shapes=[pltpu.VMEM((bm, d), F32), pltpu.VMEM((bm, d), F32)],
        compiler_params=_params("arbitrary"),
        name="gmlp_down",
    )(z, z, mu, rstd, ln_g.reshape(1, half), ln_b.reshape(1, half), w_s,
      b_s.reshape(GMLP_GROUPS, CHUNK, 1), w, x, post_g.reshape(1, d), next_g.reshape(1, d))


def _rope_table_kernel(pos_ref, inv_ref, cos_ref, sin_ref):
    ang = pos_ref[...].astype(F32) * inv_ref[...]
    cos_ref[...] = jnp.cos(ang)
    sin_ref[...] = jnp.sin(ang)


def _rope_tables(pos, inv_freq, *, bm):
    m = pos.shape[0]
    dk = inv_freq.shape[1]
    tab = pl.BlockSpec((bm, dk), lambda i: (i, 0))
    return pl.pallas_call(
        _rope_table_kernel,
        grid=(m // bm,),
        in_specs=[pl.BlockSpec((bm, 1), lambda i: (i, 0)), pl.BlockSpec((1, dk), lambda i: (0, 0))],
        out_specs=[tab, tab],
        out_shape=[jax.ShapeDtypeStruct((m, dk), F32)] * 2,
        compiler_params=_params("arbitrary"),
        name="rope_tables",
    )(pos, inv_freq)


def _ret_core_kernel(q_ref, k_ref, v_ref, g_ref, dm_ref, qd_ref, kd_ref, cd_ref, o_ref, state_ref):
    t = pl.program_id(2)

    @pl.when(t == 0)
    def _():
        state_ref[...] = jnp.zeros_like(state_ref)

    dmask = dm_ref[0]
    qdec = qd_ref[0]
    kdec = kd_ref[0]
    cdec = cd_ref[0]
    state = state_ref[...]
    for c in range(q_ref.shape[0] // CHUNK):
        rows = slice(c * CHUNK, (c + 1) * CHUNK)
        qc = q_ref[rows, :]
        kc = k_ref[rows, :]
        vc = v_ref[rows, :]
        scores = lax.dot_general(qc, kc, (((1,), (1,)), ((), ())), preferred_element_type=F32) * dmask
        intra = jnp.dot(scores.astype(BF16), vc, preferred_element_type=F32)
        qd = (qc.astype(F32) * qdec).astype(BF16)
        cross = jnp.dot(qd, state.astype(BF16), preferred_element_type=F32)
        kd = (kc.astype(F32) * kdec).astype(BF16)
        upd = lax.dot_general(kd, vc, (((0,), (0,)), ((), ())), preferred_element_type=F32)
        state = state * cdec + upd
        o = _rms(intra + cross)
        o_ref[rows, :] = (g_ref[rows, :].astype(F32) * o).astype(o_ref.dtype)
    state_ref[...] = state


def _ret_core(qk, vg, dmask, qdec, kdec, cdec, *, batch, seq, ts):
    m = qk.shape[0]
    hq = qk.shape[1] // (2 * RET_HEADS)
    hv = vg.shape[1] // (2 * RET_HEADS)
    nts = seq // ts
    return pl.pallas_call(
        _ret_core_kernel,
        grid=(batch, RET_HEADS, nts),
        in_specs=[
            pl.BlockSpec((ts, hq), lambda b, h, t: (b * nts + t, h)),
            pl.BlockSpec((ts, hq), lambda b, h, t: (b * nts + t, RET_HEADS + h)),
            pl.BlockSpec((ts, hv), lambda b, h, t: (b * nts + t, h)),
            pl.BlockSpec((ts, hv), lambda b, h, t: (b * nts + t, RET_HEADS + h)),
            pl.BlockSpec((1, CHUNK, CHUNK), lambda b, h, t: (h, 0, 0)),
            pl.BlockSpec((1, CHUNK, 1), lambda b, h, t: (h, 0, 0)),
            pl.BlockSpec((1, CHUNK, 1), lambda b, h, t: (h, 0, 0)),
            pl.BlockSpec((1, 1, 1), lambda b, h, t: (h, 0, 0)),
        ],
        out_specs=pl.BlockSpec((ts, hv), lambda b, h, t: (b * nts + t, h)),
        out_shape=jax.ShapeDtypeStruct((m, hv * RET_HEADS), BF16),
        scratch_shapes=[pltpu.VMEM((hq, hv), F32)],
        compiler_params=_params("arbitrary", "arbitrary", "arbitrary"),
        name="ret_core",
    )(qk, qk, vg, vg, dmask, qdec, kdec, cdec)


def _retention_constants(head_qk):
    log_gamma = jnp.log1p(-jnp.exp2(-5.0 - jnp.arange(RET_HEADS, dtype=F32)))
    idx = jnp.arange(CHUNK, dtype=F32)
    rel = idx[:, None] - idx[None, :]
    dmask = jnp.where(rel[None] >= 0,
                      jnp.exp(jnp.maximum(rel, 0.0)[None] * log_gamma[:, None, None]), 0.0)
    qdec = jnp.exp((idx[None, :] + 1.0) * log_gamma[:, None])[:, :, None]
    kdec = jnp.exp((CHUNK - 1.0 - idx)[None, :] * log_gamma[:, None])[:, :, None]
    cdec = jnp.exp(CHUNK * log_gamma)[:, None, None]
    inv_freq = 1.0 / (ROPE_BASE ** jnp.linspace(0.0, 1.0, head_qk // 2, dtype=F32))
    inv_freq = jnp.repeat(inv_freq, 2)[None, :]
    return dmask, qdec, kdec, cdec, inv_freq


def kernel(x, positions, mix_pre_g, mix_post_g, gmlp_w_in, gmlp_ln_g, gmlp_ln_b, gmlp_w_s, gmlp_b_s,
           gmlp_w_out, ret_w_in, ret_w_out, ffn_pre_g, ffn_post_g, ffn_w_up, ffn_conv_w, ffn_conv_b,
           ffn_w_down):
    batch, seq, d = x.shape
    m = batch * seq
    qk_dim = d
    v_dim = (ret_w_in.shape[2] - 2 * qk_dim) // 2
    head_qk = qk_dim // RET_HEADS
    xf = x.reshape(m, d)

    h = _prenorm(xf, mix_pre_g[0], bm=512)
    z, mu, rstd = _gmlp_in(h, gmlp_w_in.astype(BF16), bm=1024, bn=2048)
    xf, h = _gmlp_down(z, mu, rstd, gmlp_ln_g[0], gmlp_ln_b[0], gmlp_w_s[0], gmlp_b_s[0],
                       gmlp_w_out[0].astype(BF16), xf, mix_post_g[0], ffn_pre_g[0], bm=256)
    a = _ffn_up(h, ffn_w_up, ffn_conv_w, ffn_conv_b, 0, seq=seq, bm=1024, tn=512)
    xf, h = _down_proj(a, ffn_w_down[0].astype(BF16), xf, ffn_post_g[0], mix_pre_g[1], bm=256)

    dmask, qdec, kdec, cdec, inv_freq = _retention_constants(head_qk)
    cos, sin = _rope_tables(positions.reshape(m, 1), inv_freq, bm=1024)
    qk, vg = _ret_in(h, ret_w_in, cos, sin, bm=1024, bn=1024,
                     qk_dim=qk_dim, v_dim=v_dim, head_qk=head_qk)
    o = _ret_core(qk, vg, dmask, qdec, kdec, cdec, batch=batch, seq=seq, ts=2048)
    xf, h = _down_proj(o, ret_w_out[0].astype(BF16), xf, mix_post_g[1], ffn_pre_g[1], bm=256)
    a = _ffn_up(h, ffn_w_up, ffn_conv_w, ffn_conv_b, 1, seq=seq, bm=1024, tn=512)
    xf, _ = _down_proj(a, ffn_w_down[1].astype(BF16), xf, ffn_post_g[1], bm=256)
    return xf.reshape(batch, seq, d)
```

```python
import functools

import jax
import jax.numpy as jnp
import numpy as np
from jax import lax
from jax.experimental import pallas as pl
from jax.experimental.pallas import tpu as pltpu

EPS = 1e-6
CHUNK = 128
RET_CHUNK = 256
GMLP_GROUPS = 8
RET_HEADS = 8
ROPE_BASE = 10000.0
CONV_WIDTH = 3

F32 = jnp.float32
BF16 = jnp.bfloat16

V7X_VMEM_BYTES = 64 * 1024 * 1024
VMEM_LIMIT_BYTES = V7X_VMEM_BYTES - 8 * 1024 * 1024
LANES = 128
SUBLANES = 8


def _params(*semantics):
    return pltpu.CompilerParams(dimension_semantics=semantics, vmem_limit_bytes=VMEM_LIMIT_BYTES)


def _rms(x):
    return x * lax.rsqrt(jnp.mean(x * x, axis=-1, keepdims=True) + EPS)


def _staged_pipeline(s, n, stages):
    depth = len(stages)

    def run(active, step_parity):
        for k in active:
            stages[k]((step_parity - k) % 2)

    for fill in range(depth - 1):
        pl.when(s == fill)(functools.partial(run, range(fill + 1), fill % 2))
    for parity in (0, 1):
        pl.when((s >= depth - 1) & (s < n) & (s % 2 == parity))(
            functools.partial(run, range(depth), parity))
    for drain in range(1, depth):
        step = n - 1 + drain
        pl.when(s == step)(functools.partial(run, range(drain, depth), step % 2))


def _prenorm_kernel(x_ref, g_ref, o_ref):
    o_ref[...] = (_rms(x_ref[...]) * g_ref[...]).astype(o_ref.dtype)


def _prenorm(x, g, *, bm):
    m, d = x.shape
    return pl.pallas_call(
        _prenorm_kernel,
        grid=(m // bm,),
        in_specs=[pl.BlockSpec((bm, d), lambda i: (i, 0)), pl.BlockSpec((1, d), lambda i: (0, 0))],
        out_specs=pl.BlockSpec((bm, d), lambda i: (i, 0)),
        out_shape=jax.ShapeDtypeStruct((m, d), BF16),
        compiler_params=_params("arbitrary"),
        name="prenorm",
    )(x, g.reshape(1, d))


def _up_kernel(*refs, n_w, n_extra, n_out, n_side_in, n_side_out, weight_stationary, epilogue,
               side_fn):
    h_ref = refs[0]
    w_refs = refs[1:1 + n_w]
    pos = 1 + n_w
    extra_refs = refs[pos:pos + n_extra]
    pos += n_extra
    side_in_refs = refs[pos:pos + n_side_in]
    pos += n_side_in
    out_refs = refs[pos:pos + n_out]
    pos += n_out
    side_out_refs = refs[pos:pos + n_side_out]
    pos += n_side_out
    if weight_stationary:
        wbf_refs = refs[pos:pos + n_w]
        pos += n_w
    else:
        wbf_refs = w_refs
    epi_scratch = refs[pos:]

    if weight_stationary:
        j, i = pl.program_id(0), pl.program_id(1)

        @pl.when(i == 0)
        def _():
            for w_ref, wbf_ref in zip(w_refs, wbf_refs):
                wbf_ref[...] = w_ref[...].astype(BF16)
    else:
        i, j = pl.program_id(0), pl.program_id(1)

    if epi_scratch:
        @pl.when((i == 0) & (j == 0))
        def _():
            for ref in epi_scratch:
                ref[...] = jnp.zeros_like(ref)

    h = h_ref[...]
    ys = [jnp.dot(h, wbf_ref[...], preferred_element_type=F32) for wbf_ref in wbf_refs]
    epilogue(ys, extra_refs, out_refs, epi_scratch, i, j)
    if side_fn is not None:
        side_fn(side_in_refs, side_out_refs)


def _up_proj(h, weights, *, n_col_tiles, bm, bn, weight_stationary, epilogue, extras=(), outs,
             epi_scratch=(), side=None, name):
    m, d = h.shape
    n_i, n_j = m // bm, n_col_tiles
    if weight_stationary:
        grid = (n_j, n_i)

        def ij(fn):
            return lambda j, i: fn(i, j)

        def step(fn):
            return lambda j, i: fn(j * n_i + i)
    else:
        grid = (n_i, n_j)

        def ij(fn):
            return fn

        def step(fn):
            return lambda i, j: fn(i * n_j + j)

    side_ins, side_outs, side_fn = side(n_i * n_j) if side is not None else ((), (), None)
    in_specs = [pl.BlockSpec((bm, d), ij(lambda i, j: (i, 0)))]
    for _, layer, c0 in weights:
        in_specs.append(pl.BlockSpec((None, d, bn), ij(lambda i, j, layer=layer, c0=c0: (layer, 0, j + c0))))
    in_specs += [pl.BlockSpec(shape, ij(fn)) for _, shape, fn in extras]
    in_specs += [pl.BlockSpec(shape, step(fn)) for _, shape, fn in side_ins]
    out_specs = [pl.BlockSpec(shape, ij(fn)) for _, shape, fn in outs]
    out_specs += [pl.BlockSpec(shape, step(fn)) for _, shape, fn in side_outs]
    scratch = []
    if weight_stationary:
        scratch += [pltpu.VMEM((d, bn), BF16) for _ in weights]
    scratch += list(epi_scratch)
    kern = functools.partial(_up_kernel, n_w=len(weights), n_extra=len(extras), n_out=len(outs),
                             n_side_in=len(side_ins), n_side_out=len(side_outs),
                             weight_stationary=weight_stationary, epilogue=epilogue, side_fn=side_fn)
    res = pl.pallas_call(
        kern,
        grid=grid,
        in_specs=in_specs,
        out_specs=out_specs,
        out_shape=[sds for sds, _, _ in outs] + [sds for sds, _, _ in side_outs],
        scratch_shapes=scratch,
        compiler_params=_params("arbitrary", "arbitrary"),
        name=name,
    )(h, *[w for w, _, _ in weights], *[a for a, _, _ in extras], *[a for a, _, _ in side_ins])
    return res[:len(outs)], res[len(outs):]


def _cast_rows_side(w, layer):
    kdim, d = w.shape[1:]

    def build(n_steps):
        rows = kdim // n_steps
        assert rows * n_steps == kdim and rows % (2 * SUBLANES) == 0, (kdim, n_steps)

        def fn(in_refs, out_refs):
            out_refs[0][...] = in_refs[0][...].astype(BF16)

        return ([(w, (None, rows, d), lambda t: (layer, t, 0))],
                [(jax.ShapeDtypeStruct((kdim, d), BF16), (rows, d), lambda t: (t, 0))], fn)

    return build


def _lane_partial_sum(y):
    acc = y[:, 0:LANES]
    for c in range(1, y.shape[1] // LANES):
        acc = acc + y[:, c * LANES:(c + 1) * LANES]
    return acc


def _gelu_exact(y):
    return 0.5 * y * (1.0 + lax.erf(y * np.float32(np.sqrt(0.5))))


def _gmlp_in_epilogue(ys, extras, outs, scratch, ip, jp, *, n_u_tiles, n_tiles, v_width):
    z_ref, mu_ref, rstd_ref = outs
    s_ref, ss_ref = scratch
    z = _gelu_exact(ys[0])
    z_ref[...] = z.astype(z_ref.dtype)
    carry = jp > n_u_tiles
    in_v = jp >= n_u_tiles
    s_ref[...] = jnp.where(carry, s_ref[...], 0.0) + jnp.where(in_v, _lane_partial_sum(z), 0.0)
    ss_ref[...] = jnp.where(carry, ss_ref[...], 0.0) + jnp.where(in_v, _lane_partial_sum(z * z), 0.0)

    @pl.when(jp == n_tiles - 1)
    def _():
        mu = jnp.sum(s_ref[...], axis=-1, keepdims=True) / v_width
        var = jnp.sum(ss_ref[...], axis=-1, keepdims=True) / v_width - mu * mu
        mu_ref[...] = mu
        rstd_ref[...] = lax.rsqrt(var + EPS)


def _gmlp_in(h, w, w_out, *, bm, bn):
    m = h.shape[0]
    n = w.shape[2]
    n_tiles = n // bn
    epilogue = functools.partial(_gmlp_in_epilogue, n_u_tiles=n_tiles // 2, n_tiles=n_tiles,
                                 v_width=n // 2)
    (z, mu, rstd), (w_out_bf,) = _up_proj(
        h, [(w, 0, 0)], n_col_tiles=n_tiles, bm=bm, bn=bn, weight_stationary=False, epilogue=epilogue,
        outs=[(jax.ShapeDtypeStruct((m, n), BF16), (bm, bn), lambda i, j: (i, j)),
              (jax.ShapeDtypeStruct((m, 1), F32), (bm, 1), lambda i, j: (i, 0)),
              (jax.ShapeDtypeStruct((m, 1), F32), (bm, 1), lambda i, j: (i, 0))],
        epi_scratch=[pltpu.VMEM((bm, LANES), F32), pltpu.VMEM((bm, LANES), F32)],
        side=_cast_rows_side(w_out, 0), name="gmlp_in")
    return z, mu, rstd, w_out_bf


def _causal_conv(z, prev, cw, cb):
    row = lax.broadcasted_iota(jnp.int32, z.shape, 0)
    z1 = jnp.where(row == 0, prev[7:8, :], pltpu.roll(z, 1, 0))
    z2 = jnp.where(row == 0, prev[6:7, :], jnp.where(row == 1, prev[7:8, :], pltpu.roll(z, 2, 0)))
    return cb + cw[0:1, :] * z2 + cw[1:2, :] * z1 + cw[2:3, :] * z


def _ffn_up_epilogue(ys, extras, outs, scratch, ip, jp, *, tiles_per_seq):
    cwg_ref, cwu_ref, cbg_ref, cbu_ref = extras
    seq_start = ip % tiles_per_seq == 0

    def branch(z, cw_ref, cb_ref, carry_ref):
        prev = jnp.where(seq_start, 0.0, carry_ref[...])
        out = _causal_conv(z, prev, cw_ref[...], cb_ref[...])
        carry_ref[...] = z[z.shape[0] - SUBLANES:, :]
        return out

    gate = branch(ys[0], cwg_ref, cbg_ref, scratch[0])
    up = branch(ys[1], cwu_ref, cbu_ref, scratch[1])
    outs[0][...] = (jax.nn.silu(gate) * up).astype(outs[0].dtype)


def _ffn_up(h, w, conv_w, conv_b, w_down, layer, *, seq, bm, tn):
    m = h.shape[0]
    f = w.shape[2] // 2
    nj = f // tn
    cb = conv_b.reshape(conv_b.shape[0], 1, 2 * f)
    (a,), (w_down_bf,) = _up_proj(
        h, [(w, layer, 0), (w, layer, nj)], n_col_tiles=nj, bm=bm, bn=tn, weight_stationary=True,
        epilogue=functools.partial(_ffn_up_epilogue, tiles_per_seq=seq // bm),
        extras=[(conv_w, (None, CONV_WIDTH, tn), lambda i, j: (layer, 0, j)),
                (conv_w, (None, CONV_WIDTH, tn), lambda i, j: (layer, 0, j + nj)),
                (cb, (None, 1, tn), lambda i, j: (layer, 0, j)),
                (cb, (None, 1, tn), lambda i, j: (layer, 0, j + nj))],
        outs=[(jax.ShapeDtypeStruct((m, f), BF16), (bm, tn), lambda i, j: (i, j))],
        epi_scratch=[pltpu.VMEM((SUBLANES, tn), F32), pltpu.VMEM((SUBLANES, tn), F32)],
        side=_cast_rows_side(w_down, layer), name="ffn_up")
    return a, w_down_bf


def _rope_epilogue(ys, extras, outs, scratch, ip, jp, *, n_q_tiles, head_qk):
    cos = extras[0][...]
    sin = extras[1][...]
    y, o_ref = ys[0], outs[0]
    even = (lax.broadcasted_iota(jnp.int32, cos.shape, 1) & 1) == 0
    scale = jnp.where(jp >= n_q_tiles, head_qk ** -0.5, 1.0).astype(F32)
    for hh in range(y.shape[1] // head_qk):
        cols = slice(hh * head_qk, (hh + 1) * head_qk)
        t = y[:, cols]
        nxt = pltpu.roll(t, head_qk - 1, 1)
        prv = pltpu.roll(t, 1, 1)
        r = (t * cos + jnp.where(even, -nxt, prv) * sin) * scale
        o_ref[:, cols] = r.astype(o_ref.dtype)


def _value_gate_epilogue(ys, extras, outs, scratch, ip, jp, *, n_v_tiles):
    y = ys[0]
    outs[0][...] = jnp.where(jp >= n_v_tiles, jax.nn.silu(y), y).astype(outs[0].dtype)


def _ret_in(h, w, cos, sin, *, bm, bn, qk_dim, v_dim, head_qk):
    m = h.shape[0]
    nq = qk_dim // bn
    nv = v_dim // bn
    tab = (bm, head_qk)
    (qk,), _ = _up_proj(
        h, [(w, 0, 0)], n_col_tiles=2 * nq, bm=bm, bn=bn, weight_stationary=True,
        epilogue=functools.partial(_rope_epilogue, n_q_tiles=nq, head_qk=head_qk),
        extras=[(cos, tab, lambda i, j: (i, 0)), (sin, tab, lambda i, j: (i, 0))],
        outs=[(jax.ShapeDtypeStruct((m, 2 * qk_dim), BF16), (bm, bn), lambda i, j: (i, j))],
        name="ret_in_qk")
    (vg,), _ = _up_proj(
        h, [(w, 0, 2 * nq)], n_col_tiles=2 * nv, bm=bm, bn=bn, weight_stationary=True,
        epilogue=functools.partial(_value_gate_epilogue, n_v_tiles=nv),
        outs=[(jax.ShapeDtypeStruct((m, 2 * v_dim), BF16), (bm, bn), lambda i, j: (i, j))],
        name="ret_in_vg")
    return qk, vg


def _residual_norm_epilogue(acc, x_ref, pg_ref, ng_ref, xo_ref, ho_ref):
    xn = x_ref[...] + _rms(acc) * pg_ref[...]
    xo_ref[...] = xn
    if ho_ref is not None:
        ho_ref[...] = (_rms(xn) * ng_ref[...]).astype(ho_ref.dtype)


def _down_kernel(a_ref, w_ref, x_ref, pg_ref, *rest, with_next):
    if with_next:
        ng_ref, xo_ref, ho_ref, raw0_ref, raw1_ref = rest
    else:
        (xo_ref, raw0_ref, raw1_ref), ng_ref, ho_ref = rest, None, None
    raws = (raw0_ref, raw1_ref)

    def matmul(par):
        raws[par][...] = jnp.dot(a_ref[...], w_ref[...], preferred_element_type=F32)

    def finish(par):
        _residual_norm_epilogue(raws[par][...], x_ref, pg_ref, ng_ref, xo_ref, ho_ref)

    _staged_pipeline(pl.program_id(0), pl.num_programs(0) - 1, [matmul, finish])


def _down_proj(a, w, x, post_g, next_g=None, *, bm):
    m, kdim = a.shape
    d = w.shape[1]
    n = m // bm
    with_next = next_g is not None
    prev_row = pl.BlockSpec((bm, d), lambda s: (jnp.maximum(s - 1, 0), 0))
    gain = pl.BlockSpec((1, d), lambda s: (0, 0))
    gains = [post_g.reshape(1, d)] + ([next_g.reshape(1, d)] if with_next else [])
    out_shape = [jax.ShapeDtypeStruct((m, d), F32)]
    if with_next:
        out_shape.append(jax.ShapeDtypeStruct((m, d), BF16))
    res = pl.pallas_call(
        functools.partial(_down_kernel, with_next=with_next),
        grid=(n + 1,),
        in_specs=[
            pl.BlockSpec((bm, kdim), lambda s: (jnp.minimum(s, n - 1), 0)),
            pl.BlockSpec((kdim, d), lambda s: (0, 0)),
            prev_row,
        ] + [gain] * len(gains),
        out_specs=[prev_row] * len(out_shape),
        out_shape=out_shape,
        scratch_shapes=[pltpu.VMEM((bm, d), F32), pltpu.VMEM((bm, d), F32)],
        compiler_params=_params("arbitrary"),
        name="down_proj",
    )(a, w, x, *gains)
    return res if with_next else (res[0], None)


def _gmlp_down_kernel(u_ref, v_ref, mu_ref, rstd_ref, lg_ref, lb_ref, ws_ref, bs_ref, w_ref,
                      x_ref, pg_ref, ng_ref, xo_ref, ho_ref, raw0_ref, raw1_ref):
    raws = (raw0_ref, raw1_ref)
    bm, half = u_ref.shape
    gd = half // GMLP_GROUPS

    def gate_and_project(par):
        tril = (lax.broadcasted_iota(jnp.int32, (CHUNK, CHUNK), 0)
                >= lax.broadcasted_iota(jnp.int32, (CHUNK, CHUNK), 1))
        mu = mu_ref[...]
        rstd = rstd_ref[...]
        acc = None
        for g in range(GMLP_GROUPS):
            cols = slice(g * gd, (g + 1) * gd)
            ws = jnp.where(tril, ws_ref[g], 0.0).astype(BF16)
            bias = bs_ref[g]
            vn = (v_ref[:, cols].astype(F32) - mu) * rstd * lg_ref[:, cols] + lb_ref[:, cols]
            vn = vn.astype(BF16)
            gated = []
            for c in range(bm // CHUNK):
                rows = slice(c * CHUNK, (c + 1) * CHUNK)
                mixed = jnp.dot(ws, vn[rows], preferred_element_type=F32) + bias
                gated.append((u_ref[rows, cols].astype(F32) * mixed).astype(BF16))
            part = jnp.dot(jnp.concatenate(gated, axis=0), w_ref[cols, :], preferred_element_type=F32)
            acc = part if acc is None else acc + part
        raws[par][...] = acc

    def finish(par):
        _residual_norm_epilogue(raws[par][...], x_ref, pg_ref, ng_ref, xo_ref, ho_ref)

    _staged_pipeline(pl.program_id(0), pl.num_programs(0) - 1, [gate_and_project, finish])


def _gmlp_down(z, mu, rstd, ln_g, ln_b, w_s, b_s, w, x, post_g, next_g, *, bm):
    m = z.shape[0]
    half, d = w.shape
    n = m // bm

    def first(s):
        return jnp.minimum(s, n - 1)

    def last(s):
        return jnp.maximum(s - 1, 0)

    row = pl.BlockSpec((bm, d), lambda s: (last(s), 0))
    gain = pl.BlockSpec((1, d), lambda s: (0, 0))
    stat = pl.BlockSpec((bm, 1), lambda s: (first(s), 0))
    lnp = pl.BlockSpec((1, half), lambda s: (0, 0))
    return pl.pallas_call(
        _gmlp_down_kernel,
        grid=(n + 1,),
        in_specs=[
            pl.BlockSpec((bm, half), lambda s: (first(s), 0)),
            pl.BlockSpec((bm, half), lambda s: (first(s), 1)),
            stat, stat, lnp, lnp,
            pl.BlockSpec((GMLP_GROUPS, CHUNK, CHUNK), lambda s: (0, 0, 0)),
            pl.BlockSpec((GMLP_GROUPS, CHUNK, 1), lambda s: (0, 0, 0)),
            pl.BlockSpec((half, d), lambda s: (0, 0)),
            row, gain, gain,
        ],
        out_specs=[row, row],
        out_shape=[jax.ShapeDtypeStruct((m, d), F32), jax.ShapeDtypeStruct((m, d), BF16)],
        scratch_shapes=[pltpu.VMEM((bm, d), F32), pltpu.VMEM((bm, d), F32)],
        compiler_params=_params("arbitrary"),
        name="gmlp_down",
    )(z, z, mu, rstd, ln_g.reshape(1, half), ln_b.reshape(1, half), w_s,
      b_s.reshape(GMLP_GROUPS, CHUNK, 1), w, x, post_g.reshape(1, d), next_g.reshape(1, d))


def _rope_table_kernel(pos_ref, inv_ref, cos_ref, sin_ref):
    ang = pos_ref[...].astype(F32) * inv_ref[...]
    cos_ref[...] = jnp.cos(ang)
    sin_ref[...] = jnp.sin(ang)


def _rope_tables(pos, inv_freq, *, bm):
    m = pos.shape[0]
    dk = inv_freq.shape[1]
    tab = pl.BlockSpec((bm, dk), lambda i: (i, 0))
    return pl.pallas_call(
        _rope_table_kernel,
        grid=(m // bm,),
        in_specs=[pl.BlockSpec((bm, 1), lambda i: (i, 0)), pl.BlockSpec((1, dk), lambda i: (0, 0))],
        out_specs=[tab, tab],
        out_shape=[jax.ShapeDtypeStruct((m, dk), F32)] * 2,
        compiler_params=_params("arbitrary"),
        name="rope_tables",
    )(pos, inv_freq)


def _ret_core_kernel(q_ref, k_ref, v_ref, g_ref, dm_ref, qd_ref, kd_ref, cd_ref, wsrc_ref,
                     o_ref, wdst_ref, state_ref):
    t = pl.program_id(2)

    @pl.when(t == 0)
    def _():
        state_ref[...] = jnp.zeros_like(state_ref)

    wdst_ref[...] = wsrc_ref[...].astype(BF16)
    dmask = dm_ref[0]
    qdec = qd_ref[0]
    kdec = kd_ref[0]
    cdec = cd_ref[0]
    chunk = dmask.shape[0]
    state = state_ref[...]
    for c in range(q_ref.shape[0] // chunk):
        rows = slice(c * chunk, (c + 1) * chunk)
        qc = q_ref[rows, :]
        kc = k_ref[rows, :]
        vc = v_ref[rows, :]
        scores = lax.dot_general(qc, kc, (((1,), (1,)), ((), ())), preferred_element_type=F32) * dmask
        intra = jnp.dot(scores.astype(BF16), vc, preferred_element_type=F32)
        qd = (qc.astype(F32) * qdec).astype(BF16)
        cross = jnp.dot(qd, state.astype(BF16), preferred_element_type=F32)
        kd = (kc.astype(F32) * kdec).astype(BF16)
        upd = lax.dot_general(kd, vc, (((0,), (0,)), ((), ())), preferred_element_type=F32)
        state = state * cdec + upd
        o = _rms(intra + cross)
        o_ref[rows, :] = (g_ref[rows, :].astype(F32) * o).astype(o_ref.dtype)
    state_ref[...] = state


def _ret_core(qk, vg, dmask, qdec, kdec, cdec, w_out, *, batch, seq, ts):
    m = qk.shape[0]
    hq = qk.shape[1] // (2 * RET_HEADS)
    hv = vg.shape[1] // (2 * RET_HEADS)
    chunk = dmask.shape[1]
    nts = seq // ts
    kdim, d = w_out.shape[1:]
    n_steps = batch * RET_HEADS * nts
    wrows = kdim // n_steps
    assert wrows * n_steps == kdim and wrows % (2 * SUBLANES) == 0, (kdim, n_steps)

    def step(b, h, t):
        return (b * RET_HEADS + h) * nts + t

    return pl.pallas_call(
        _ret_core_kernel,
        grid=(batch, RET_HEADS, nts),
        in_specs=[
            pl.BlockSpec((ts, hq), lambda b, h, t: (b * nts + t, h)),
            pl.BlockSpec((ts, hq), lambda b, h, t: (b * nts + t, RET_HEADS + h)),
            pl.BlockSpec((ts, hv), lambda b, h, t: (b * nts + t, h)),
            pl.BlockSpec((ts, hv), lambda b, h, t: (b * nts + t, RET_HEADS + h)),
            pl.BlockSpec((1, chunk, chunk), lambda b, h, t: (h, 0, 0)),
            pl.BlockSpec((1, chunk, 1), lambda b, h, t: (h, 0, 0)),
            pl.BlockSpec((1, chunk, 1), lambda b, h, t: (h, 0, 0)),
            pl.BlockSpec((1, 1, 1), lambda b, h, t: (h, 0, 0)),
            pl.BlockSpec((None, wrows, d), lambda b, h, t: (0, step(b, h, t), 0)),
        ],
        out_specs=[pl.BlockSpec((ts, hv), lambda b, h, t: (b * nts + t, h)),
                   pl.BlockSpec((wrows, d), lambda b, h, t: (step(b, h, t), 0))],
        out_shape=[jax.ShapeDtypeStruct((m, hv * RET_HEADS), BF16),
                   jax.ShapeDtypeStruct((kdim, d), BF16)],
        scratch_shapes=[pltpu.VMEM((hq, hv), F32)],
        compiler_params=_params("arbitrary", "arbitrary", "arbitrary"),
        name="ret_core",
    )(qk, qk, vg, vg, dmask, qdec, kdec, cdec, w_out)


def _retention_constants(head_qk, chunk):
    log_gamma = jnp.log1p(-jnp.exp2(-5.0 - jnp.arange(RET_HEADS, dtype=F32)))
    idx = jnp.arange(chunk, dtype=F32)
    rel = idx[:, None] - idx[None, :]
    dmask = jnp.where(rel[None] >= 0,
                      jnp.exp(jnp.maximum(rel, 0.0)[None] * log_gamma[:, None, None]), 0.0)
    qdec = jnp.exp((idx[None, :] + 1.0) * log_gamma[:, None])[:, :, None]
    kdec = jnp.exp((chunk - 1.0 - idx)[None, :] * log_gamma[:, None])[:, :, None]
    cdec = jnp.exp(chunk * log_gamma)[:, None, None]
    inv_freq = 1.0 / (ROPE_BASE ** jnp.linspace(0.0, 1.0, head_qk // 2, dtype=F32))
    inv_freq = jnp.repeat(inv_freq, 2)[None, :]
    return dmask, qdec, kdec, cdec, inv_freq


def kernel(x, positions, mix_pre_g, mix_post_g, gmlp_w_in, gmlp_ln_g, gmlp_ln_b, gmlp_w_s, gmlp_b_s,
           gmlp_w_out, ret_w_in, ret_w_out, ffn_pre_g, ffn_post_g, ffn_w_up, ffn_conv_w, ffn_conv_b,
           ffn_w_down):
    batch, seq, d = x.shape
    m = batch * seq
    qk_dim = d
    v_dim = (ret_w_in.shape[2] - 2 * qk_dim) // 2
    head_qk = qk_dim // RET_HEADS
    xf = x.reshape(m, d)

    h = _prenorm(xf, mix_pre_g[0], bm=512)
    z, mu, rstd, w_bf = _gmlp_in(h, gmlp_w_in.astype(BF16), gmlp_w_out, bm=1024, bn=2048)
    xf, h = _gmlp_down(z, mu, rstd, gmlp_ln_g[0], gmlp_ln_b[0], gmlp_w_s[0], gmlp_b_s[0],
                       w_bf, xf, mix_post_g[0], ffn_pre_g[0], bm=256)
    a, w_bf = _ffn_up(h, ffn_w_up, ffn_conv_w, ffn_conv_b, ffn_w_down, 0, seq=seq, bm=1024, tn=512)
    xf, h = _down_proj(a, w_bf, xf, ffn_post_g[0], mix_pre_g[1], bm=256)

    dmask, qdec, kdec, cdec, inv_freq = _retention_constants(head_qk, RET_CHUNK)
    cos, sin = _rope_tables(positions.reshape(m, 1), inv_freq, bm=1024)
    qk, vg = _ret_in(h, ret_w_in, cos, sin, bm=1024, bn=1024,
                     qk_dim=qk_dim, v_dim=v_dim, head_qk=head_qk)
    o, w_bf = _ret_core(qk, vg, dmask, qdec, kdec, cdec, ret_w_out, batch=batch, seq=seq, ts=2048)
    xf, h = _down_proj(o, w_bf, xf, mix_post_g[1], ffn_pre_g[1], bm=256)
    a, w_bf = _ffn_up(h, ffn_w_up, ffn_conv_w, ffn_conv_b, ffn_w_down, 1, seq=seq, bm=1024, tn=512)
    xf, _ = _down_proj(a, w_bf, xf, ffn_post_g[1], bm=256)
    return xf.reshape(batch, seq, d)
```

```python
import functools

import jax
import jax.numpy as jnp
import numpy as np
from jax import lax
from jax.experimental import pallas as pl
from jax.experimental.pallas import tpu as pltpu

EPS = 1e-6
CHUNK = 128
RET_CHUNK = 256
GMLP_GROUPS = 8
RET_HEADS = 8
ROPE_BASE = 10000.0
CONV_WIDTH = 3

F32 = jnp.float32
BF16 = jnp.bfloat16

V7X_VMEM_BYTES = 64 * 1024 * 1024
VMEM_LIMIT_BYTES = V7X_VMEM_BYTES - 8 * 1024 * 1024
LANES = 128
SUBLANES = 8


def _params(*semantics):
    return pltpu.CompilerParams(dimension_semantics=semantics, vmem_limit_bytes=VMEM_LIMIT_BYTES)


def _rms(x):
    return x * lax.rsqrt(jnp.mean(x * x, axis=-1, keepdims=True) + EPS)


def _staged_pipeline(s, n, stages):
    depth = len(stages)

    def run(active, step_parity):
        for k in active:
            stages[k]((step_parity - k) % 2)

    for fill in range(depth - 1):
        pl.when(s == fill)(functools.partial(run, range(fill + 1), fill % 2))
    for parity in (0, 1):
        pl.when((s >= depth - 1) & (s < n) & (s % 2 == parity))(
            functools.partial(run, range(depth), parity))
    for drain in range(1, depth):
        step = n - 1 + drain
        pl.when(s == step)(functools.partial(run, range(drain, depth), step % 2))


def _prenorm_kernel(x_ref, g_ref, o_ref):
    o_ref[...] = (_rms(x_ref[...]) * g_ref[...]).astype(o_ref.dtype)


def _prenorm(x, g, *, bm):
    m, d = x.shape
    return pl.pallas_call(
        _prenorm_kernel,
        grid=(m // bm,),
        in_specs=[pl.BlockSpec((bm, d), lambda i: (i, 0)), pl.BlockSpec((1, d), lambda i: (0, 0))],
        out_specs=pl.BlockSpec((bm, d), lambda i: (i, 0)),
        out_shape=jax.ShapeDtypeStruct((m, d), BF16),
        compiler_params=_params("arbitrary"),
        name="prenorm",
    )(x, g.reshape(1, d))


def _up_kernel(*refs, n_w, n_extra, n_out, side_jobs, weight_stationary, cast_weights, epilogue):
    h_ref = refs[0]
    w_refs = refs[1:1 + n_w]
    pos = 1 + n_w
    extra_refs = refs[pos:pos + n_extra]
    pos += n_extra
    side_in_refs = []
    for n_in, _, _ in side_jobs:
        side_in_refs.append(refs[pos:pos + n_in])
        pos += n_in
    out_refs = refs[pos:pos + n_out]
    pos += n_out
    side_out_refs = []
    for _, n_o, _ in side_jobs:
        side_out_refs.append(refs[pos:pos + n_o])
        pos += n_o
    if cast_weights:
        wbf_refs = refs[pos:pos + n_w]
        pos += n_w
    else:
        wbf_refs = w_refs
    epi_scratch = refs[pos:]

    if weight_stationary:
        j, i = pl.program_id(0), pl.program_id(1)
    else:
        i, j = pl.program_id(0), pl.program_id(1)

    if cast_weights:
        @pl.when(i == 0)
        def _():
            for w_ref, wbf_ref in zip(w_refs, wbf_refs):
                wbf_ref[...] = w_ref[...].astype(BF16)

    if epi_scratch:
        @pl.when((i == 0) & (j == 0))
        def _():
            for ref in epi_scratch:
                ref[...] = jnp.zeros_like(ref)

    h = h_ref[...]
    ys = [jnp.dot(h, wbf_ref[...], preferred_element_type=F32) for wbf_ref in wbf_refs]
    epilogue(ys, extra_refs, out_refs, epi_scratch, i, j)
    for (_, _, fn), ins, outs in zip(side_jobs, side_in_refs, side_out_refs):
        fn(ins, outs)


def _up_proj(h, weights, *, n_col_tiles, bm, bn, weight_stationary, epilogue, extras=(), outs,
             epi_scratch=(), sides=(), name):
    m, d = h.shape
    n_i, n_j = m // bm, n_col_tiles
    cast_weights = weights[0][0].dtype != BF16
    assert weight_stationary or not cast_weights
    if weight_stationary:
        grid = (n_j, n_i)

        def ij(fn):
            return lambda j, i: fn(i, j)

        def step(fn):
            return lambda j, i: fn(j * n_i + i)
    else:
        grid = (n_i, n_j)

        def ij(fn):
            return fn

        def step(fn):
            return lambda i, j: fn(i * n_j + j)

    built = [side(n_i * n_j) for side in sides]
    side_ins = [entry for ins, _, _ in built for entry in ins]
    side_outs = [entry for _, outs_, _ in built for entry in outs_]
    in_specs = [pl.BlockSpec((bm, d), ij(lambda i, j: (i, 0)))]
    for _, layer, c0 in weights:
        in_specs.append(pl.BlockSpec((None, d, bn), ij(lambda i, j, layer=layer, c0=c0: (layer, 0, j + c0))))
    in_specs += [pl.BlockSpec(shape, ij(fn)) for _, shape, fn in extras]
    in_specs += [pl.BlockSpec(shape, step(fn)) for _, shape, fn in side_ins]
    out_specs = [pl.BlockSpec(shape, ij(fn)) for _, shape, fn in outs]
    out_specs += [pl.BlockSpec(shape, step(fn)) for _, shape, fn in side_outs]
    scratch = []
    if cast_weights:
        scratch += [pltpu.VMEM((d, bn), BF16) for _ in weights]
    scratch += list(epi_scratch)
    kern = functools.partial(_up_kernel, n_w=len(weights), n_extra=len(extras), n_out=len(outs),
                             side_jobs=[(len(ins), len(outs_), fn) for ins, outs_, fn in built],
                             weight_stationary=weight_stationary, cast_weights=cast_weights,
                             epilogue=epilogue)
    res = pl.pallas_call(
        kern,
        grid=grid,
        in_specs=in_specs,
        out_specs=out_specs,
        out_shape=[sds for sds, _, _ in outs] + [sds for sds, _, _ in side_outs],
        scratch_shapes=scratch,
        compiler_params=_params("arbitrary", "arbitrary"),
        name=name,
    )(h, *[w for w, _, _ in weights], *[a for a, _, _ in extras], *[a for a, _, _ in side_ins])
    return res[:len(outs)], res[len(outs):]


def _cast_rows_side(w, layer, *, n_used=None, keep_layer_axis=False):
    kdim, d = w.shape[1:]

    def build(n_steps):
        used = n_steps if n_used is None else n_used
        rows = kdim // used
        assert used <= n_steps and rows * used == kdim and rows % (2 * SUBLANES) == 0, (kdim, used)

        def fn(in_refs, out_refs):
            out_refs[0][...] = in_refs[0][...].astype(BF16)

        def block(t):
            return jnp.minimum(t, used - 1)

        if keep_layer_axis:
            out = (jax.ShapeDtypeStruct((1, kdim, d), BF16), (None, rows, d), lambda t: (0, block(t), 0))
        else:
            out = (jax.ShapeDtypeStruct((kdim, d), BF16), (rows, d), lambda t: (block(t), 0))
        return [(w, (None, rows, d), lambda t: (layer, block(t), 0))], [out], fn

    return build


def _lane_partial_sum(y):
    acc = y[:, 0:LANES]
    for c in range(1, y.shape[1] // LANES):
        acc = acc + y[:, c * LANES:(c + 1) * LANES]
    return acc


def _gelu_exact(y):
    return 0.5 * y * (1.0 + lax.erf(y * np.float32(np.sqrt(0.5))))


def _gmlp_in_epilogue(ys, extras, outs, scratch, ip, jp, *, n_u_tiles, n_tiles, v_width):
    z_ref, mu_ref, rstd_ref = outs
    s_ref, ss_ref = scratch
    z = _gelu_exact(ys[0])
    z_ref[...] = z.astype(z_ref.dtype)
    carry = jp > n_u_tiles
    in_v = jp >= n_u_tiles
    s_ref[...] = jnp.where(carry, s_ref[...], 0.0) + jnp.where(in_v, _lane_partial_sum(z), 0.0)
    ss_ref[...] = jnp.where(carry, ss_ref[...], 0.0) + jnp.where(in_v, _lane_partial_sum(z * z), 0.0)

    @pl.when(jp == n_tiles - 1)
    def _():
        mu = jnp.sum(s_ref[...], axis=-1, keepdims=True) / v_width
        var = jnp.sum(ss_ref[...], axis=-1, keepdims=True) / v_width - mu * mu
        mu_ref[...] = mu
        rstd_ref[...] = lax.rsqrt(var + EPS)


def _gmlp_in(h, w, w_out, *, bm, bn):
    m = h.shape[0]
    n = w.shape[2]
    n_tiles = n // bn
    epilogue = functools.partial(_gmlp_in_epilogue, n_u_tiles=n_tiles // 2, n_tiles=n_tiles,
                                 v_width=n // 2)
    (z, mu, rstd), (w_out_bf,) = _up_proj(
        h, [(w, 0, 0)], n_col_tiles=n_tiles, bm=bm, bn=bn, weight_stationary=False, epilogue=epilogue,
        outs=[(jax.ShapeDtypeStruct((m, n), BF16), (bm, bn), lambda i, j: (i, j)),
              (jax.ShapeDtypeStruct((m, 1), F32), (bm, 1), lambda i, j: (i, 0)),
              (jax.ShapeDtypeStruct((m, 1), F32), (bm, 1), lambda i, j: (i, 0))],
        epi_scratch=[pltpu.VMEM((bm, LANES), F32), pltpu.VMEM((bm, LANES), F32)],
        sides=[_cast_rows_side(w_out, 0)], name="gmlp_in")
    return z, mu, rstd, w_out_bf


def _causal_conv(z, prev, cw, cb):
    row = lax.broadcasted_iota(jnp.int32, z.shape, 0)
    z1 = jnp.where(row == 0, prev[7:8, :], pltpu.roll(z, 1, 0))
    z2 = jnp.where(row == 0, prev[6:7, :], jnp.where(row == 1, prev[7:8, :], pltpu.roll(z, 2, 0)))
    return cb + cw[0:1, :] * z2 + cw[1:2, :] * z1 + cw[2:3, :] * z


def _ffn_up_epilogue(ys, extras, outs, scratch, ip, jp, *, tiles_per_seq):
    cwg_ref, cwu_ref, cbg_ref, cbu_ref = extras
    seq_start = ip % tiles_per_seq == 0

    def branch(z, cw_ref, cb_ref, carry_ref):
        prev = jnp.where(seq_start, 0.0, carry_ref[...])
        out = _causal_conv(z, prev, cw_ref[...], cb_ref[...])
        carry_ref[...] = z[z.shape[0] - SUBLANES:, :]
        return out

    gate = branch(ys[0], cwg_ref, cbg_ref, scratch[0])
    up = branch(ys[1], cwu_ref, cbu_ref, scratch[1])
    outs[0][...] = (jax.nn.silu(gate) * up).astype(outs[0].dtype)


def _ffn_up(h, w, conv_w, conv_b, w_down, layer, *, seq, bm, tn, more_sides=()):
    m = h.shape[0]
    f = w.shape[2] // 2
    nj = f // tn
    cb = conv_b.reshape(conv_b.shape[0], 1, 2 * f)
    (a,), side_outs = _up_proj(
        h, [(w, layer, 0), (w, layer, nj)], n_col_tiles=nj, bm=bm, bn=tn, weight_stationary=True,
        epilogue=functools.partial(_ffn_up_epilogue, tiles_per_seq=seq // bm),
        extras=[(conv_w, (None, CONV_WIDTH, tn), lambda i, j: (layer, 0, j)),
                (conv_w, (None, CONV_WIDTH, tn), lambda i, j: (layer, 0, j + nj)),
                (cb, (None, 1, tn), lambda i, j: (layer, 0, j)),
                (cb, (None, 1, tn), lambda i, j: (layer, 0, j + nj))],
        outs=[(jax.ShapeDtypeStruct((m, f), BF16), (bm, tn), lambda i, j: (i, j))],
        epi_scratch=[pltpu.VMEM((SUBLANES, tn), F32), pltpu.VMEM((SUBLANES, tn), F32)],
        sides=[_cast_rows_side(w_down, layer), *more_sides], name="ffn_up")
    return (a, *side_outs)


def _rope_epilogue(ys, extras, outs, scratch, ip, jp, *, n_q_tiles, head_qk):
    cos = extras[0][...]
    sin = extras[1][...]
    y, o_ref = ys[0], outs[0]
    even = (lax.broadcasted_iota(jnp.int32, cos.shape, 1) & 1) == 0
    scale = jnp.where(jp >= n_q_tiles, head_qk ** -0.5, 1.0).astype(F32)
    for hh in range(y.shape[1] // head_qk):
        cols = slice(hh * head_qk, (hh + 1) * head_qk)
        t = y[:, cols]
        nxt = pltpu.roll(t, head_qk - 1, 1)
        prv = pltpu.roll(t, 1, 1)
        r = (t * cos + jnp.where(even, -nxt, prv) * sin) * scale
        o_ref[:, cols] = r.astype(o_ref.dtype)


def _value_gate_epilogue(ys, extras, outs, scratch, ip, jp, *, n_v_tiles):
    y = ys[0]
    outs[0][...] = jnp.where(jp >= n_v_tiles, jax.nn.silu(y), y).astype(outs[0].dtype)


def _ret_in(h, w, cos, sin, *, bm, bn, qk_dim, v_dim, head_qk):
    m = h.shape[0]
    nq = qk_dim // bn
    nv = v_dim // bn
    tab = (bm, head_qk)
    (qk,), _ = _up_proj(
        h, [(w, 0, 0)], n_col_tiles=2 * nq, bm=bm, bn=bn, weight_stationary=True,
        epilogue=functools.partial(_rope_epilogue, n_q_tiles=nq, head_qk=head_qk),
        extras=[(cos, tab, lambda i, j: (i, 0)), (sin, tab, lambda i, j: (i, 0))],
        outs=[(jax.ShapeDtypeStruct((m, 2 * qk_dim), BF16), (bm, bn), lambda i, j: (i, j))],
        name="ret_in_qk")
    (vg,), _ = _up_proj(
        h, [(w, 0, 2 * nq)], n_col_tiles=2 * nv, bm=bm, bn=bn, weight_stationary=True,
        epilogue=functools.partial(_value_gate_epilogue, n_v_tiles=nv),
        outs=[(jax.ShapeDtypeStruct((m, 2 * v_dim), BF16), (bm, bn), lambda i, j: (i, j))],
        name="ret_in_vg")
    return qk, vg


def _residual_norm_epilogue(acc, x_ref, pg_ref, ng_ref, xo_ref, ho_ref):
    xn = x_ref[...] + _rms(acc) * pg_ref[...]
    xo_ref[...] = xn
    if ho_ref is not None:
        ho_ref[...] = (_rms(xn) * ng_ref[...]).astype(ho_ref.dtype)


def _down_kernel(a_ref, w_ref, x_ref, pg_ref, *rest, with_next):
    if with_next:
        ng_ref, xo_ref, ho_ref, raw0_ref, raw1_ref = rest
    else:
        (xo_ref, raw0_ref, raw1_ref), ng_ref, ho_ref = rest, None, None
    raws = (raw0_ref, raw1_ref)

    def matmul(par):
        raws[par][...] = jnp.dot(a_ref[...], w_ref[...], preferred_element_type=F32)

    def finish(par):
        _residual_norm_epilogue(raws[par][...], x_ref, pg_ref, ng_ref, xo_ref, ho_ref)

    _staged_pipeline(pl.program_id(0), pl.num_programs(0) - 1, [matmul, finish])


def _down_proj(a, w, x, post_g, next_g=None, *, bm):
    m, kdim = a.shape
    d = w.shape[1]
    n = m // bm
    with_next = next_g is not None
    prev_row = pl.BlockSpec((bm, d), lambda s: (jnp.maximum(s - 1, 0), 0))
    gain = pl.BlockSpec((1, d), lambda s: (0, 0))
    gains = [post_g.reshape(1, d)] + ([next_g.reshape(1, d)] if with_next else [])
    out_shape = [jax.ShapeDtypeStruct((m, d), F32)]
    if with_next:
        out_shape.append(jax.ShapeDtypeStruct((m, d), BF16))
    res = pl.pallas_call(
        functools.partial(_down_kernel, with_next=with_next),
        grid=(n + 1,),
        in_specs=[
            pl.BlockSpec((bm, kdim), lambda s: (jnp.minimum(s, n - 1), 0)),
            pl.BlockSpec((kdim, d), lambda s: (0, 0)),
            prev_row,
        ] + [gain] * len(gains),
        out_specs=[prev_row] * len(out_shape),
        out_shape=out_shape,
        scratch_shapes=[pltpu.VMEM((bm, d), F32), pltpu.VMEM((bm, d), F32)],
        compiler_params=_params("arbitrary"),
        name="down_proj",
    )(a, w, x, *gains)
    return res if with_next else (res[0], None)


def _gmlp_down_kernel(u_ref, v_ref, mu_ref, rstd_ref, lg_ref, lb_ref, ws_ref, bs_ref, w_ref,
                      x_ref, pg_ref, ng_ref, xo_ref, ho_ref, raw0_ref, raw1_ref):
    raws = (raw0_ref, raw1_ref)
    bm, half = u_ref.shape
    gd = half // GMLP_GROUPS

    def gate_and_project(par):
        tril = (lax.broadcasted_iota(jnp.int32, (CHUNK, CHUNK), 0)
                >= lax.broadcasted_iota(jnp.int32, (CHUNK, CHUNK), 1))
        mu = mu_ref[...]
        rstd = rstd_ref[...]
        acc = None
        for g in range(GMLP_GROUPS):
            cols = slice(g * gd, (g + 1) * gd)
            ws = jnp.where(tril, ws_ref[g], 0.0).astype(BF16)
            bias = bs_ref[g]
            vn = (v_ref[:, cols].astype(F32) - mu) * rstd * lg_ref[:, cols] + lb_ref[:, cols]
            vn = vn.astype(BF16)
            gated = []
            for c in range(bm // CHUNK):
                rows = slice(c * CHUNK, (c + 1) * CHUNK)
                mixed = jnp.dot(ws, vn[rows], preferred_element_type=F32) + bias
                gated.append((u_ref[rows, cols].astype(F32) * mixed).astype(BF16))
            part = jnp.dot(jnp.concatenate(gated, axis=0), w_ref[cols, :], preferred_element_type=F32)
            acc = part if acc is None else acc + part
        raws[par][...] = acc

    def finish(par):
        _residual_norm_epilogue(raws[par][...], x_ref, pg_ref, ng_ref, xo_ref, ho_ref)

    _staged_pipeline(pl.program_id(0), pl.num_programs(0) - 1, [gate_and_project, finish])


def _gmlp_down(z, mu, rstd, ln_g, ln_b, w_s, b_s, w, x, post_g, next_g, *, bm):
    m = z.shape[0]
    half, d = w.shape
    n = m // bm

    def first(s):
        return jnp.minimum(s, n - 1)

    def last(s):
        return jnp.maximum(s - 1, 0)

    row = pl.BlockSpec((bm, d), lambda s: (last(s), 0))
    gain = pl.BlockSpec((1, d), lambda s: (0, 0))
    stat = pl.BlockSpec((bm, 1), lambda s: (first(s), 0))
    lnp = pl.BlockSpec((1, half), lambda s: (0, 0))
    return pl.pallas_call(
        _gmlp_down_kernel,
        grid=(n + 1,),
        in_specs=[
            pl.BlockSpec((bm, half), lambda s: (first(s), 0)),
            pl.BlockSpec((bm, half), lambda s: (first(s), 1)),
            stat, stat, lnp, lnp,
            pl.BlockSpec((GMLP_GROUPS, CHUNK, CHUNK), lambda s: (0, 0, 0)),
            pl.BlockSpec((GMLP_GROUPS, CHUNK, 1), lambda s: (0, 0, 0)),
            pl.BlockSpec((half, d), lambda s: (0, 0)),
            row, gain, gain,
        ],
        out_specs=[row, row],
        out_shape=[jax.ShapeDtypeStruct((m, d), F32), jax.ShapeDtypeStruct((m, d), BF16)],
        scratch_shapes=[pltpu.VMEM((bm, d), F32), pltpu.VMEM((bm, d), F32)],
        compiler_params=_params("arbitrary"),
        name="gmlp_down",
    )(z, z, mu, rstd, ln_g.reshape(1, half), ln_b.reshape(1, half), w_s,
      b_s.reshape(GMLP_GROUPS, CHUNK, 1), w, x, post_g.reshape(1, d), next_g.reshape(1, d))


def _rope_table_kernel(pos_ref, inv_ref, cos_ref, sin_ref):
    ang = pos_ref[...].astype(F32) * inv_ref[...]
    cos_ref[...] = jnp.cos(ang)
    sin_ref[...] = jnp.sin(ang)


def _rope_tables(pos, inv_freq, *, bm):
    m = pos.shape[0]
    dk = inv_freq.shape[1]
    tab = pl.BlockSpec((bm, dk), lambda i: (i, 0))
    return pl.pallas_call(
        _rope_table_kernel,
        grid=(m // bm,),
        in_specs=[pl.BlockSpec((bm, 1), lambda i: (i, 0)), pl.BlockSpec((1, dk), lambda i: (0, 0))],
        out_specs=[tab, tab],
        out_shape=[jax.ShapeDtypeStruct((m, dk), F32)] * 2,
        compiler_params=_params("arbitrary"),
        name="rope_tables",
    )(pos, inv_freq)


def _ret_core_kernel(q_ref, k_ref, v_ref, g_ref, dm_ref, qd_ref, kd_ref, cd_ref, wsrc_ref,
                     o_ref, wdst_ref, state_ref):
    t = pl.program_id(2)

    @pl.when(t == 0)
    def _():
        state_ref[...] = jnp.zeros_like(state_ref)

    wdst_ref[...] = wsrc_ref[...].astype(BF16)
    dmask = dm_ref[0]
    qdec = qd_ref[0]
    kdec = kd_ref[0]
    cdec = cd_ref[0]
    chunk = dmask.shape[0]
    state = state_ref[...]
    for c in range(q_ref.shape[0] // chunk):
        rows = slice(c * chunk, (c + 1) * chunk)
        qc = q_ref[rows, :]
        kc = k_ref[rows, :]
        vc = v_ref[rows, :]
        scores = lax.dot_general(qc, kc, (((1,), (1,)), ((), ())), preferred_element_type=F32) * dmask
        intra = jnp.dot(scores.astype(BF16), vc, preferred_element_type=F32)
        qd = (qc.astype(F32) * qdec).astype(BF16)
        cross = jnp.dot(qd, state.astype(BF16), preferred_element_type=F32)
        kd = (kc.astype(F32) * kdec).astype(BF16)
        upd = lax.dot_general(kd, vc, (((0,), (0,)), ((), ())), preferred_element_type=F32)
        state = state * cdec + upd
        o = _rms(intra + cross)
        o_ref[rows, :] = (g_ref[rows, :].astype(F32) * o).astype(o_ref.dtype)
    state_ref[...] = state


def _ret_core(qk, vg, dmask, qdec, kdec, cdec, w_out, *, batch, seq, ts):
    m = qk.shape[0]
    hq = qk.shape[1] // (2 * RET_HEADS)
    hv = vg.shape[1] // (2 * RET_HEADS)
    chunk = dmask.shape[1]
    nts = seq // ts
    kdim, d = w_out.shape[1:]
    n_steps = batch * RET_HEADS * nts
    wrows = kdim // n_steps
    assert wrows * n_steps == kdim and wrows % (2 * SUBLANES) == 0, (kdim, n_steps)

    def step(b, h, t):
        return (b * RET_HEADS + h) * nts + t

    return pl.pallas_call(
        _ret_core_kernel,
        grid=(batch, RET_HEADS, nts),
        in_specs=[
            pl.BlockSpec((ts, hq), lambda b, h, t: (b * nts + t, h)),
            pl.BlockSpec((ts, hq), lambda b, h, t: (b * nts + t, RET_HEADS + h)),
            pl.BlockSpec((ts, hv), lambda b, h, t: (b * nts + t, h)),
            pl.BlockSpec((ts, hv), lambda b, h, t: (b * nts + t, RET_HEADS + h)),
            pl.BlockSpec((1, chunk, chunk), lambda b, h, t: (h, 0, 0)),
            pl.BlockSpec((1, chunk, 1), lambda b, h, t: (h, 0, 0)),
            pl.BlockSpec((1, chunk, 1), lambda b, h, t: (h, 0, 0)),
            pl.BlockSpec((1, 1, 1), lambda b, h, t: (h, 0, 0)),
            pl.BlockSpec((None, wrows, d), lambda b, h, t: (0, step(b, h, t), 0)),
        ],
        out_specs=[pl.BlockSpec((ts, hv), lambda b, h, t: (b * nts + t, h)),
                   pl.BlockSpec((wrows, d), lambda b, h, t: (step(b, h, t), 0))],
        out_shape=[jax.ShapeDtypeStruct((m, hv * RET_HEADS), BF16),
                   jax.ShapeDtypeStruct((kdim, d), BF16)],
        scratch_shapes=[pltpu.VMEM((hq, hv), F32)],
        compiler_params=_params("arbitrary", "arbitrary", "arbitrary"),
        name="ret_core",
    )(qk, qk, vg, vg, dmask, qdec, kdec, cdec, w_out)


def _retention_constants(head_qk, chunk):
    log_gamma = jnp.log1p(-jnp.exp2(-5.0 - jnp.arange(RET_HEADS, dtype=F32)))
    idx = jnp.arange(chunk, dtype=F32)
    rel = idx[:, None] - idx[None, :]
    dmask = jnp.where(rel[None] >= 0,
                      jnp.exp(jnp.maximum(rel, 0.0)[None] * log_gamma[:, None, None]), 0.0)
    qdec = jnp.exp((idx[None, :] + 1.0) * log_gamma[:, None])[:, :, None]
    kdec = jnp.exp((chunk - 1.0 - idx)[None, :] * log_gamma[:, None])[:, :, None]
    cdec = jnp.exp(chunk * log_gamma)[:, None, None]
    inv_freq = 1.0 / (ROPE_BASE ** jnp.linspace(0.0, 1.0, head_qk // 2, dtype=F32))
    inv_freq = jnp.repeat(inv_freq, 2)[None, :]
    return dmask, qdec, kdec, cdec, inv_freq


def kernel(x, positions, mix_pre_g, mix_post_g, gmlp_w_in, gmlp_ln_g, gmlp_ln_b, gmlp_w_s, gmlp_b_s,
           gmlp_w_out, ret_w_in, ret_w_out, ffn_pre_g, ffn_post_g, ffn_w_up, ffn_conv_w, ffn_conv_b,
           ffn_w_down):
    batch, seq, d = x.shape
    m = batch * seq
    qk_dim = d
    v_dim = (ret_w_in.shape[2] - 2 * qk_dim) // 2
    head_qk = qk_dim // RET_HEADS
    xf = x.reshape(m, d)

    h = _prenorm(xf, mix_pre_g[0], bm=512)
    z, mu, rstd, w_bf = _gmlp_in(h, gmlp_w_in.astype(BF16), gmlp_w_out, bm=1024, bn=2048)
    xf, h = _gmlp_down(z, mu, rstd, gmlp_ln_g[0], gmlp_ln_b[0], gmlp_w_s[0], gmlp_b_s[0],
                       w_bf, xf, mix_post_g[0], ffn_pre_g[0], bm=256)
    ret_w_in_side = _cast_rows_side(ret_w_in, 0, n_used=128, keep_layer_axis=True)
    a, w_bf, ret_w_in_bf = _ffn_up(h, ffn_w_up, ffn_conv_w, ffn_conv_b, ffn_w_down, 0, seq=seq, bm=1024,
                                   tn=512, more_sides=[ret_w_in_side])
    xf, h = _down_proj(a, w_bf, xf, ffn_post_g[0], mix_pre_g[1], bm=256)

    dmask, qdec, kdec, cdec, inv_freq = _retention_constants(head_qk, RET_CHUNK)
    cos, sin = _rope_tables(positions.reshape(m, 1), inv_freq, bm=1024)
    qk, vg = _ret_in(h, ret_w_in_bf, cos, sin, bm=1024, bn=2048,
                     qk_dim=qk_dim, v_dim=v_dim, head_qk=head_qk)
    o, w_bf = _ret_core(qk, vg, dmask, qdec, kdec, cdec, ret_w_out, batch=batch, seq=seq, ts=4096)
    xf, h = _down_proj(o, w_bf, xf, mix_post_g[1], ffn_pre_g[1], bm=256)
    a, w_bf = _ffn_up(h, ffn_w_up, ffn_conv_w, ffn_conv_b, ffn_w_down, 1, seq=seq, bm=1024, tn=512)
    xf, _ = _down_proj(a, w_bf, xf, ffn_post_g[1], bm=256)
    return xf.reshape(batch, seq, d)
```

```python
import functools

import jax
import jax.numpy as jnp
import numpy as np
from jax import lax
from jax.experimental import pallas as pl
from jax.experimental.pallas import tpu as pltpu

EPS = 1e-6
CHUNK = 128
RET_CHUNK = 256
GMLP_GROUPS = 8
RET_HEADS = 8
ROPE_BASE = 10000.0
CONV_WIDTH = 3

F32 = jnp.float32
BF16 = jnp.bfloat16

V7X_VMEM_BYTES = 64 * 1024 * 1024
VMEM_LIMIT_BYTES = V7X_VMEM_BYTES - 8 * 1024 * 1024
LANES = 128
SUBLANES = 8


def _params(*semantics):
    return pltpu.CompilerParams(dimension_semantics=semantics, vmem_limit_bytes=VMEM_LIMIT_BYTES)


def _rms(x):
    return x * lax.rsqrt(jnp.mean(x * x, axis=-1, keepdims=True) + EPS)


def _staged_pipeline(s, n, stages):
    depth = len(stages)

    def run(active, step_parity):
        for k in active:
            stages[k]((step_parity - k) % 2)

    for fill in range(depth - 1):
        pl.when(s == fill)(functools.partial(run, range(fill + 1), fill % 2))
    for parity in (0, 1):
        pl.when((s >= depth - 1) & (s < n) & (s % 2 == parity))(
            functools.partial(run, range(depth), parity))
    for drain in range(1, depth):
        step = n - 1 + drain
        pl.when(s == step)(functools.partial(run, range(drain, depth), step % 2))


def _prenorm_kernel(x_ref, g_ref, o_ref):
    o_ref[...] = (_rms(x_ref[...]) * g_ref[...]).astype(o_ref.dtype)


def _prenorm(x, g, *, bm):
    m, d = x.shape
    return pl.pallas_call(
        _prenorm_kernel,
        grid=(m // bm,),
        in_specs=[pl.BlockSpec((bm, d), lambda i: (i, 0)), pl.BlockSpec((1, d), lambda i: (0, 0))],
        out_specs=pl.BlockSpec((bm, d), lambda i: (i, 0)),
        out_shape=jax.ShapeDtypeStruct((m, d), BF16),
        compiler_params=_params("arbitrary"),
        name="prenorm",
    )(x, g.reshape(1, d))


def _up_kernel(*refs, n_w, n_extra, n_out, side_jobs, weight_stationary, cast_weights, epilogue):
    h_ref = refs[0]
    w_refs = refs[1:1 + n_w]
    pos = 1 + n_w
    extra_refs = refs[pos:pos + n_extra]
    pos += n_extra
    side_in_refs = []
    for n_in, _, _ in side_jobs:
        side_in_refs.append(refs[pos:pos + n_in])
        pos += n_in
    out_refs = refs[pos:pos + n_out]
    pos += n_out
    side_out_refs = []
    for _, n_o, _ in side_jobs:
        side_out_refs.append(refs[pos:pos + n_o])
        pos += n_o
    if cast_weights:
        wbf_refs = refs[pos:pos + n_w]
        pos += n_w
    else:
        wbf_refs = w_refs
    epi_scratch = refs[pos:]

    if weight_stationary:
        j, i = pl.program_id(0), pl.program_id(1)
    else:
        i, j = pl.program_id(0), pl.program_id(1)

    if cast_weights:
        @pl.when(i == 0)
        def _():
            for w_ref, wbf_ref in zip(w_refs, wbf_refs):
                wbf_ref[...] = w_ref[...].astype(BF16)

    if epi_scratch:
        @pl.when((i == 0) & (j == 0))
        def _():
            for ref in epi_scratch:
                ref[...] = jnp.zeros_like(ref)

    h = h_ref[...]
    ys = [jnp.dot(h, wbf_ref[...], preferred_element_type=F32) for wbf_ref in wbf_refs]
    epilogue(ys, extra_refs, out_refs, epi_scratch, i, j)
    for (_, _, fn), ins, outs in zip(side_jobs, side_in_refs, side_out_refs):
        fn(ins, outs)


def _up_proj(h, weights, *, n_col_tiles, bm, bn, weight_stationary, epilogue, extras=(), outs,
             epi_scratch=(), sides=(), name):
    m, d = h.shape
    n_i, n_j = m // bm, n_col_tiles
    cast_weights = weights[0][0].dtype != BF16
    assert weight_stationary or not cast_weights
    if weight_stationary:
        grid = (n_j, n_i)

        def ij(fn):
            return lambda j, i: fn(i, j)

        def step(fn):
            return lambda j, i: fn(j * n_i + i)
    else:
        grid = (n_i, n_j)

        def ij(fn):
            return fn

        def step(fn):
            return lambda i, j: fn(i * n_j + j)

    built = [side(n_i * n_j) for side in sides]
    side_ins = [entry for ins, _, _ in built for entry in ins]
    side_outs = [entry for _, outs_, _ in built for entry in outs_]
    in_specs = [pl.BlockSpec((bm, d), ij(lambda i, j: (i, 0)))]
    for _, layer, c0 in weights:
        in_specs.append(pl.BlockSpec((None, d, bn), ij(lambda i, j, layer=layer, c0=c0: (layer, 0, j + c0))))
    in_specs += [pl.BlockSpec(shape, ij(fn)) for _, shape, fn in extras]
    in_specs += [pl.BlockSpec(shape, step(fn)) for _, shape, fn in side_ins]
    out_specs = [pl.BlockSpec(shape, ij(fn)) for _, shape, fn in outs]
    out_specs += [pl.BlockSpec(shape, step(fn)) for _, shape, fn in side_outs]
    scratch = []
    if cast_weights:
        scratch += [pltpu.VMEM((d, bn), BF16) for _ in weights]
    scratch += list(epi_scratch)
    kern = functools.partial(_up_kernel, n_w=len(weights), n_extra=len(extras), n_out=len(outs),
                             side_jobs=[(len(ins), len(outs_), fn) for ins, outs_, fn in built],
                             weight_stationary=weight_stationary, cast_weights=cast_weights,
                             epilogue=epilogue)
    res = pl.pallas_call(
        kern,
        grid=grid,
        in_specs=in_specs,
        out_specs=out_specs,
        out_shape=[sds for sds, _, _ in outs] + [sds for sds, _, _ in side_outs],
        scratch_shapes=scratch,
        compiler_params=_params("arbitrary", "arbitrary"),
        name=name,
    )(h, *[w for w, _, _ in weights], *[a for a, _, _ in extras], *[a for a, _, _ in side_ins])
    return res[:len(outs)], res[len(outs):]


def _cast_rows_side(w, layer, *, n_used=None, keep_layer_axis=False):
    kdim, d = w.shape[1:]

    def build(n_steps):
        used = n_steps if n_used is None else n_used
        rows = kdim // used
        assert used <= n_steps and rows * used == kdim and rows % (2 * SUBLANES) == 0, (kdim, used)

        def fn(in_refs, out_refs):
            out_refs[0][...] = in_refs[0][...].astype(BF16)

        def block(t):
            return jnp.minimum(t, used - 1)

        if keep_layer_axis:
            out = (jax.ShapeDtypeStruct((1, kdim, d), BF16), (None, rows, d), lambda t: (0, block(t), 0))
        else:
            out = (jax.ShapeDtypeStruct((kdim, d), BF16), (rows, d), lambda t: (block(t), 0))
        return [(w, (None, rows, d), lambda t: (layer, block(t), 0))], [out], fn

    return build


def _lane_partial_sum(y):
    acc = y[:, 0:LANES]
    for c in range(1, y.shape[1] // LANES):
        acc = acc + y[:, c * LANES:(c + 1) * LANES]
    return acc


def _gelu_exact(y):
    return 0.5 * y * (1.0 + lax.erf(y * np.float32(np.sqrt(0.5))))


def _gmlp_in_epilogue(ys, extras, outs, scratch, ip, jp, *, n_u_tiles, n_tiles, v_width):
    z_ref, mu_ref, rstd_ref = outs
    s_ref, ss_ref = scratch
    z = _gelu_exact(ys[0])
    z_ref[...] = z.astype(z_ref.dtype)
    carry = jp > n_u_tiles
    in_v = jp >= n_u_tiles
    s_ref[...] = jnp.where(carry, s_ref[...], 0.0) + jnp.where(in_v, _lane_partial_sum(z), 0.0)
    ss_ref[...] = jnp.where(carry, ss_ref[...], 0.0) + jnp.where(in_v, _lane_partial_sum(z * z), 0.0)

    @pl.when(jp == n_tiles - 1)
    def _():
        mu = jnp.sum(s_ref[...], axis=-1, keepdims=True) / v_width
        var = jnp.sum(ss_ref[...], axis=-1, keepdims=True) / v_width - mu * mu
        mu_ref[...] = mu
        rstd_ref[...] = lax.rsqrt(var + EPS)


def _gmlp_in(h, w, w_out, *, bm, bn):
    m = h.shape[0]
    n = w.shape[2]
    n_tiles = n // bn
    epilogue = functools.partial(_gmlp_in_epilogue, n_u_tiles=n_tiles // 2, n_tiles=n_tiles,
                                 v_width=n // 2)
    (z, mu, rstd), (w_out_bf,) = _up_proj(
        h, [(w, 0, 0)], n_col_tiles=n_tiles, bm=bm, bn=bn, weight_stationary=False, epilogue=epilogue,
        outs=[(jax.ShapeDtypeStruct((m, n), BF16), (bm, bn), lambda i, j: (i, j)),
              (jax.ShapeDtypeStruct((m, 1), F32), (bm, 1), lambda i, j: (i, 0)),
              (jax.ShapeDtypeStruct((m, 1), F32), (bm, 1), lambda i, j: (i, 0))],
        epi_scratch=[pltpu.VMEM((bm, LANES), F32), pltpu.VMEM((bm, LANES), F32)],
        sides=[_cast_rows_side(w_out, 0)], name="gmlp_in")
    return z, mu, rstd, w_out_bf


def _causal_conv(z, prev, cw, cb):
    row = lax.broadcasted_iota(jnp.int32, z.shape, 0)
    z1 = jnp.where(row == 0, prev[7:8, :], pltpu.roll(z, 1, 0))
    z2 = jnp.where(row == 0, prev[6:7, :], jnp.where(row == 1, prev[7:8, :], pltpu.roll(z, 2, 0)))
    return cb + cw[0:1, :] * z2 + cw[1:2, :] * z1 + cw[2:3, :] * z


def _ffn_up_epilogue(ys, extras, outs, scratch, ip, jp, *, tiles_per_seq):
    cwg_ref, cwu_ref, cbg_ref, cbu_ref = extras
    seq_start = ip % tiles_per_seq == 0

    def branch(z, cw_ref, cb_ref, carry_ref):
        prev = jnp.where(seq_start, 0.0, carry_ref[...])
        out = _causal_conv(z, prev, cw_ref[...], cb_ref[...])
        carry_ref[...] = z[z.shape[0] - SUBLANES:, :]
        return out

    gate = branch(ys[0], cwg_ref, cbg_ref, scratch[0])
    up = branch(ys[1], cwu_ref, cbu_ref, scratch[1])
    outs[0][...] = (jax.nn.silu(gate) * up).astype(outs[0].dtype)


def _ffn_up(h, w, conv_w, conv_b, w_down, layer, *, seq, bm, tn, more_sides=()):
    m = h.shape[0]
    f = w.shape[2] // 2
    nj = f // tn
    cb = conv_b.reshape(conv_b.shape[0], 1, 2 * f)
    (a,), side_outs = _up_proj(
        h, [(w, layer, 0), (w, layer, nj)], n_col_tiles=nj, bm=bm, bn=tn, weight_stationary=True,
        epilogue=functools.partial(_ffn_up_epilogue, tiles_per_seq=seq // bm),
        extras=[(conv_w, (None, CONV_WIDTH, tn), lambda i, j: (layer, 0, j)),
                (conv_w, (None, CONV_WIDTH, tn), lambda i, j: (layer, 0, j + nj)),
                (cb, (None, 1, tn), lambda i, j: (layer, 0, j)),
                (cb, (None, 1, tn), lambda i, j: (layer, 0, j + nj))],
        outs=[(jax.ShapeDtypeStruct((m, f), BF16), (bm, tn), lambda i, j: (i, j))],
        epi_scratch=[pltpu.VMEM((SUBLANES, tn), F32), pltpu.VMEM((SUBLANES, tn), F32)],
        sides=[_cast_rows_side(w_down, layer), *more_sides], name="ffn_up")
    return (a, *side_outs)


def _rope_epilogue(ys, extras, outs, scratch, ip, jp, *, n_q_tiles, head_qk):
    cos = extras[0][...]
    sin = extras[1][...]
    y, o_ref = ys[0], outs[0]
    even = (lax.broadcasted_iota(jnp.int32, cos.shape, 1) & 1) == 0
    scale = jnp.where(jp >= n_q_tiles, head_qk ** -0.5, 1.0).astype(F32)
    for hh in range(y.shape[1] // head_qk):
        t = y[:, hh * head_qk:(hh + 1) * head_qk]
        nxt = pltpu.roll(t, head_qk - 1, 1)
        prv = pltpu.roll(t, 1, 1)
        r = (t * cos + jnp.where(even, -nxt, prv) * sin) * scale
        o_ref[hh] = r.astype(o_ref.dtype)


def _value_gate_epilogue(ys, extras, outs, scratch, ip, jp, *, n_v_tiles):
    o_ref = outs[0]
    head_v = o_ref.shape[2]
    y = ys[0]
    y = jnp.where(jp >= n_v_tiles, jax.nn.silu(y), y).astype(o_ref.dtype)
    for hh in range(o_ref.shape[0]):
        o_ref[hh] = y[:, hh * head_v:(hh + 1) * head_v]


def _ret_in(h, w, cos, sin, *, bm, bn, qk_dim, v_dim, head_qk):
    m = h.shape[0]
    nq = qk_dim // bn
    nv = v_dim // bn
    head_v = v_dim // RET_HEADS
    tab = (bm, head_qk)
    qk_heads = bn // head_qk
    vg_heads = bn // head_v
    (qk,), _ = _up_proj(
        h, [(w, 0, 0)], n_col_tiles=2 * nq, bm=bm, bn=bn, weight_stationary=True,
        epilogue=functools.partial(_rope_epilogue, n_q_tiles=nq, head_qk=head_qk),
        extras=[(cos, tab, lambda i, j: (i, 0)), (sin, tab, lambda i, j: (i, 0))],
        outs=[(jax.ShapeDtypeStruct((2 * RET_HEADS, m, head_qk), BF16), (qk_heads, bm, head_qk),
               lambda i, j: (j, i, 0))],
        name="ret_in_qk")
    (vg,), _ = _up_proj(
        h, [(w, 0, 2 * nq)], n_col_tiles=2 * nv, bm=bm, bn=bn, weight_stationary=True,
        epilogue=functools.partial(_value_gate_epilogue, n_v_tiles=nv),
        outs=[(jax.ShapeDtypeStruct((2 * RET_HEADS, m, head_v), BF16), (vg_heads, bm, head_v),
               lambda i, j: (j, i, 0))],
        name="ret_in_vg")
    return qk, vg


def _residual_norm_epilogue(acc, x_ref, pg_ref, ng_ref, xo_ref, ho_ref):
    xn = x_ref[...] + _rms(acc) * pg_ref[...]
    xo_ref[...] = xn
    if ho_ref is not None:
        ho_ref[...] = (_rms(xn) * ng_ref[...]).astype(ho_ref.dtype)


def _down_kernel(a_ref, w_ref, x_ref, pg_ref, *rest, with_next):
    if with_next:
        ng_ref, xo_ref, ho_ref, raw0_ref, raw1_ref = rest
    else:
        (xo_ref, raw0_ref, raw1_ref), ng_ref, ho_ref = rest, None, None
    raws = (raw0_ref, raw1_ref)

    def matmul(par):
        raws[par][...] = jnp.dot(a_ref[...], w_ref[...], preferred_element_type=F32)

    def finish(par):
        _residual_norm_epilogue(raws[par][...], x_ref, pg_ref, ng_ref, xo_ref, ho_ref)

    _staged_pipeline(pl.program_id(0), pl.num_programs(0) - 1, [matmul, finish])


def _down_proj(a, w, x, post_g, next_g=None, *, bm):
    m, kdim = a.shape
    d = w.shape[1]
    n = m // bm
    with_next = next_g is not None
    prev_row = pl.BlockSpec((bm, d), lambda s: (jnp.maximum(s - 1, 0), 0))
    gain = pl.BlockSpec((1, d), lambda s: (0, 0))
    gains = [post_g.reshape(1, d)] + ([next_g.reshape(1, d)] if with_next else [])
    out_shape = [jax.ShapeDtypeStruct((m, d), F32)]
    if with_next:
        out_shape.append(jax.ShapeDtypeStruct((m, d), BF16))
    res = pl.pallas_call(
        functools.partial(_down_kernel, with_next=with_next),
        grid=(n + 1,),
        in_specs=[
            pl.BlockSpec((bm, kdim), lambda s: (jnp.minimum(s, n - 1), 0)),
            pl.BlockSpec((kdim, d), lambda s: (0, 0)),
            prev_row,
        ] + [gain] * len(gains),
        out_specs=[prev_row] * len(out_shape),
        out_shape=out_shape,
        scratch_shapes=[pltpu.VMEM((bm, d), F32), pltpu.VMEM((bm, d), F32)],
        compiler_params=_params("arbitrary"),
        name="down_proj",
    )(a, w, x, *gains)
    return res if with_next else (res[0], None)


def _gmlp_down_kernel(u_ref, v_ref, mu_ref, rstd_ref, lg_ref, lb_ref, ws_ref, bs_ref, w_ref,
                      x_ref, pg_ref, ng_ref, xo_ref, ho_ref, raw0_ref, raw1_ref):
    raws = (raw0_ref, raw1_ref)
    bm, half = u_ref.shape
    gd = half // GMLP_GROUPS

    def gate_and_project(par):
        tril = (lax.broadcasted_iota(jnp.int32, (CHUNK, CHUNK), 0)
                >= lax.broadcasted_iota(jnp.int32, (CHUNK, CHUNK), 1))
        mu = mu_ref[...]
        rstd = rstd_ref[...]
        acc = None
        for g in range(GMLP_GROUPS):
            cols = slice(g * gd, (g + 1) * gd)
            ws = jnp.where(tril, ws_ref[g], 0.0).astype(BF16)
            bias = bs_ref[g]
            vn = (v_ref[:, cols].astype(F32) - mu) * rstd * lg_ref[:, cols] + lb_ref[:, cols]
            vn = vn.astype(BF16)
            gated = []
            for c in range(bm // CHUNK):
                rows = slice(c * CHUNK, (c + 1) * CHUNK)
                mixed = jnp.dot(ws, vn[rows], preferred_element_type=F32) + bias
                gated.append((u_ref[rows, cols].astype(F32) * mixed).astype(BF16))
            part = jnp.dot(jnp.concatenate(gated, axis=0), w_ref[cols, :], preferred_element_type=F32)
            acc = part if acc is None else acc + part
        raws[par][...] = acc

    def finish(par):
        _residual_norm_epilogue(raws[par][...], x_ref, pg_ref, ng_ref, xo_ref, ho_ref)

    _staged_pipeline(pl.program_id(0), pl.num_programs(0) - 1, [gate_and_project, finish])


def _gmlp_down(z, mu, rstd, ln_g, ln_b, w_s, b_s, w, x, post_g, next_g, *, bm):
    m = z.shape[0]
    half, d = w.shape
    n = m // bm

    def first(s):
        return jnp.minimum(s, n - 1)

    def last(s):
        return jnp.maximum(s - 1, 0)

    row = pl.BlockSpec((bm, d), lambda s: (last(s), 0))
    gain = pl.BlockSpec((1, d), lambda s: (0, 0))
    stat = pl.BlockSpec((bm, 1), lambda s: (first(s), 0))
    lnp = pl.BlockSpec((1, half), lambda s: (0, 0))
    return pl.pallas_call(
        _gmlp_down_kernel,
        grid=(n + 1,),
        in_specs=[
            pl.BlockSpec((bm, half), lambda s: (first(s), 0)),
            pl.BlockSpec((bm, half), lambda s: (first(s), 1)),
            stat, stat, lnp, lnp,
            pl.BlockSpec((GMLP_GROUPS, CHUNK, CHUNK), lambda s: (0, 0, 0)),
            pl.BlockSpec((GMLP_GROUPS, CHUNK, 1), lambda s: (0, 0, 0)),
            pl.BlockSpec((half, d), lambda s: (0, 0)),
            row, gain, gain,
        ],
        out_specs=[row, row],
        out_shape=[jax.ShapeDtypeStruct((m, d), F32), jax.ShapeDtypeStruct((m, d), BF16)],
        scratch_shapes=[pltpu.VMEM((bm, d), F32), pltpu.VMEM((bm, d), F32)],
        compiler_params=_params("arbitrary"),
        name="gmlp_down",
    )(z, z, mu, rstd, ln_g.reshape(1, half), ln_b.reshape(1, half), w_s,
      b_s.reshape(GMLP_GROUPS, CHUNK, 1), w, x, post_g.reshape(1, d), next_g.reshape(1, d))


def _rope_table_kernel(pos_ref, inv_ref, cos_ref, sin_ref):
    ang = pos_ref[...].astype(F32) * inv_ref[...]
    cos_ref[...] = jnp.cos(ang)
    sin_ref[...] = jnp.sin(ang)


def _rope_tables(pos, inv_freq, *, bm):
    m = pos.shape[0]
    dk = inv_freq.shape[1]
    tab = pl.BlockSpec((bm, dk), lambda i: (i, 0))
    return pl.pallas_call(
        _rope_table_kernel,
        grid=(m // bm,),
        in_specs=[pl.BlockSpec((bm, 1), lambda i: (i, 0)), pl.BlockSpec((1, dk), lambda i: (0, 0))],
        out_specs=[tab, tab],
        out_shape=[jax.ShapeDtypeStruct((m, dk), F32)] * 2,
        compiler_params=_params("arbitrary"),
        name="rope_tables",
    )(pos, inv_freq)


def _ret_core_kernel(q_ref, k_ref, v_ref, g_ref, dm_ref, qd_ref, kd_ref, cd_ref, wsrc_ref,
                     o_ref, wdst_ref, state_ref):
    t = pl.program_id(2)

    @pl.when(t == 0)
    def _():
        state_ref[...] = jnp.zeros_like(state_ref)

    wdst_ref[...] = wsrc_ref[...].astype(BF16)
    dmask = dm_ref[0]
    qdec = qd_ref[0]
    kdec = kd_ref[0]
    cdec = cd_ref[0]
    chunk = dmask.shape[0]
    state = state_ref[...]
    for c in range(q_ref.shape[0] // chunk):
        rows = slice(c * chunk, (c + 1) * chunk)
        qc = q_ref[rows, :]
        kc = k_ref[rows, :]
        vc = v_ref[rows, :]
        scores = lax.dot_general(qc, kc, (((1,), (1,)), ((), ())), preferred_element_type=F32) * dmask
        intra = jnp.dot(scores.astype(BF16), vc, preferred_element_type=F32)
        qd = (qc.astype(F32) * qdec).astype(BF16)
        cross = jnp.dot(qd, state.astype(BF16), preferred_element_type=F32)
        kd = (kc.astype(F32) * kdec).astype(BF16)
        upd = lax.dot_general(kd, vc, (((0,), (0,)), ((), ())), preferred_element_type=F32)
        state = state * cdec + upd
        o = _rms(intra + cross)
        o_ref[rows, :] = (g_ref[rows, :].astype(F32) * o).astype(o_ref.dtype)
    state_ref[...] = state


def _ret_core(qk, vg, dmask, qdec, kdec, cdec, w_out, *, batch, seq, ts):
    m, hq = qk.shape[1:]
    hv = vg.shape[2]
    chunk = dmask.shape[1]
    nts = seq // ts
    kdim, d = w_out.shape[1:]
    n_steps = batch * RET_HEADS * nts
    wrows = kdim // n_steps
    assert wrows * n_steps == kdim and wrows % (2 * SUBLANES) == 0, (kdim, n_steps)

    def step(b, h, t):
        return (b * RET_HEADS + h) * nts + t

    return pl.pallas_call(
        _ret_core_kernel,
        grid=(batch, RET_HEADS, nts),
        in_specs=[
            pl.BlockSpec((None, ts, hq), lambda b, h, t: (h, b * nts + t, 0)),
            pl.BlockSpec((None, ts, hq), lambda b, h, t: (RET_HEADS + h, b * nts + t, 0)),
            pl.BlockSpec((None, ts, hv), lambda b, h, t: (h, b * nts + t, 0)),
            pl.BlockSpec((None, ts, hv), lambda b, h, t: (RET_HEADS + h, b * nts + t, 0)),
            pl.BlockSpec((1, chunk, chunk), lambda b, h, t: (h, 0, 0)),
            pl.BlockSpec((1, chunk, 1), lambda b, h, t: (h, 0, 0)),
            pl.BlockSpec((1, chunk, 1), lambda b, h, t: (h, 0, 0)),
            pl.BlockSpec((1, 1, 1), lambda b, h, t: (h, 0, 0)),
            pl.BlockSpec((None, wrows, d), lambda b, h, t: (0, step(b, h, t), 0)),
        ],
        out_specs=[pl.BlockSpec((ts, hv), lambda b, h, t: (b * nts + t, h)),
                   pl.BlockSpec((wrows, d), lambda b, h, t: (step(b, h, t), 0))],
        out_shape=[jax.ShapeDtypeStruct((m, hv * RET_HEADS), BF16),
                   jax.ShapeDtypeStruct((kdim, d), BF16)],
        scratch_shapes=[pltpu.VMEM((hq, hv), F32)],
        compiler_params=_params("arbitrary", "arbitrary", "arbitrary"),
        name="ret_core",
    )(qk, qk, vg, vg, dmask, qdec, kdec, cdec, w_out)


def _retention_constants(head_qk, chunk):
    log_gamma = jnp.log1p(-jnp.exp2(-5.0 - jnp.arange(RET_HEADS, dtype=F32)))
    idx = jnp.arange(chunk, dtype=F32)
    rel = idx[:, None] - idx[None, :]
    dmask = jnp.where(rel[None] >= 0,
                      jnp.exp(jnp.maximum(rel, 0.0)[None] * log_gamma[:, None, None]), 0.0)
    qdec = jnp.exp((idx[None, :] + 1.0) * log_gamma[:, None])[:, :, None]
    kdec = jnp.exp((chunk - 1.0 - idx)[None, :] * log_gamma[:, None])[:, :, None]
    cdec = jnp.exp(chunk * log_gamma)[:, None, None]
    inv_freq = 1.0 / (ROPE_BASE ** jnp.linspace(0.0, 1.0, head_qk // 2, dtype=F32))
    inv_freq = jnp.repeat(inv_freq, 2)[None, :]
    return dmask, qdec, kdec, cdec, inv_freq


def kernel(x, positions, mix_pre_g, mix_post_g, gmlp_w_in, gmlp_ln_g, gmlp_ln_b, gmlp_w_s, gmlp_b_s,
           gmlp_w_out, ret_w_in, ret_w_out, ffn_pre_g, ffn_post_g, ffn_w_up, ffn_conv_w, ffn_conv_b,
           ffn_w_down):
    batch, seq, d = x.shape
    m = batch * seq
    qk_dim = d
    v_dim = (ret_w_in.shape[2] - 2 * qk_dim) // 2
    head_qk = qk_dim // RET_HEADS
    xf = x.reshape(m, d)

    h = _prenorm(xf, mix_pre_g[0], bm=512)
    z, mu, rstd, w_bf = _gmlp_in(h, gmlp_w_in.astype(BF16), gmlp_w_out, bm=1024, bn=2048)
    xf, h = _gmlp_down(z, mu, rstd, gmlp_ln_g[0], gmlp_ln_b[0], gmlp_w_s[0], gmlp_b_s[0],
                       w_bf, xf, mix_post_g[0], ffn_pre_g[0], bm=256)
    ret_w_in_side = _cast_rows_side(ret_w_in, 0, n_used=128, keep_layer_axis=True)
    a, w_bf, ret_w_in_bf = _ffn_up(h, ffn_w_up, ffn_conv_w, ffn_conv_b, ffn_w_down, 0, seq=seq, bm=1024,
                                   tn=512, more_sides=[ret_w_in_side])
    xf, h = _down_proj(a, w_bf, xf, ffn_post_g[0], mix_pre_g[1], bm=256)

    dmask, qdec, kdec, cdec, inv_freq = _retention_constants(head_qk, RET_CHUNK)
    cos, sin = _rope_tables(positions.reshape(m, 1), inv_freq, bm=1024)
    qk, vg = _ret_in(h, ret_w_in_bf, cos, sin, bm=1024, bn=2048,
                     qk_dim=qk_dim, v_dim=v_dim, head_qk=head_qk)
    o, w_bf = _ret_core(qk, vg, dmask, qdec, kdec, cdec, ret_w_out, batch=batch, seq=seq, ts=4096)
    xf, h = _down_proj(o, w_bf, xf, mix_post_g[1], ffn_pre_g[1], bm=256)
    a, w_bf = _ffn_up(h, ffn_w_up, ffn_conv_w, ffn_conv_b, ffn_w_down, 1, seq=seq, bm=1024, tn=512)
    xf, _ = _down_proj(a, w_bf, xf, ffn_post_g[1], bm=256)
    return xf.reshape(batch, seq, d)
```

```python
import functools

import jax
import jax.numpy as jnp
import numpy as np
from jax import lax
from jax.experimental import pallas as pl
from jax.experimental.pallas import tpu as pltpu

EPS = 1e-6
CHUNK = 128
RET_CHUNK = 256
GMLP_GROUPS = 8
RET_HEADS = 8
ROPE_BASE = 10000.0
CONV_WIDTH = 3

F32 = jnp.float32
BF16 = jnp.bfloat16

V7X_VMEM_BYTES = 64 * 1024 * 1024
VMEM_LIMIT_BYTES = V7X_VMEM_BYTES - 8 * 1024 * 1024
LANES = 128
SUBLANES = 8


def _params(*semantics):
    return pltpu.CompilerParams(dimension_semantics=semantics, vmem_limit_bytes=VMEM_LIMIT_BYTES)


def _rms(x):
    return x * lax.rsqrt(jnp.mean(x * x, axis=-1, keepdims=True) + EPS)


def _staged_pipeline(s, n, stages):
    depth = len(stages)

    def run(active, step_parity):
        for k in active:
            stages[k]((step_parity - k) % 2)

    for fill in range(depth - 1):
        pl.when(s == fill)(functools.partial(run, range(fill + 1), fill % 2))
    for parity in (0, 1):
        pl.when((s >= depth - 1) & (s < n) & (s % 2 == parity))(
            functools.partial(run, range(depth), parity))
    for drain in range(1, depth):
        step = n - 1 + drain
        pl.when(s == step)(functools.partial(run, range(drain, depth), step % 2))


def _prenorm_kernel(x_ref, g_ref, o_ref):
    o_ref[...] = (_rms(x_ref[...]) * g_ref[...]).astype(o_ref.dtype)


def _prenorm(x, g, *, bm):
    m, d = x.shape
    return pl.pallas_call(
        _prenorm_kernel,
        grid=(m // bm,),
        in_specs=[pl.BlockSpec((bm, d), lambda i: (i, 0)), pl.BlockSpec((1, d), lambda i: (0, 0))],
        out_specs=pl.BlockSpec((bm, d), lambda i: (i, 0)),
        out_shape=jax.ShapeDtypeStruct((m, d), BF16),
        compiler_params=_params("arbitrary"),
        name="prenorm",
    )(x, g.reshape(1, d))


def _up_kernel(*refs, n_w, n_extra, n_out, side_jobs, weight_stationary, cast_weights, row_subtiles,
               epilogue):
    h_ref = refs[0]
    w_refs = refs[1:1 + n_w]
    pos = 1 + n_w
    extra_refs = refs[pos:pos + n_extra]
    pos += n_extra
    side_in_refs = []
    for n_in, _, _ in side_jobs:
        side_in_refs.append(refs[pos:pos + n_in])
        pos += n_in
    out_refs = refs[pos:pos + n_out]
    pos += n_out
    side_out_refs = []
    for _, n_o, _ in side_jobs:
        side_out_refs.append(refs[pos:pos + n_o])
        pos += n_o
    if cast_weights:
        wbf_refs = refs[pos:pos + n_w]
        pos += n_w
    else:
        wbf_refs = w_refs
    epi_scratch = refs[pos:]

    if weight_stationary:
        j, i = pl.program_id(0), pl.program_id(1)
    else:
        i, j = pl.program_id(0), pl.program_id(1)

    if cast_weights:
        @pl.when(i == 0)
        def _():
            for w_ref, wbf_ref in zip(w_refs, wbf_refs):
                wbf_ref[...] = w_ref[...].astype(BF16)

    if epi_scratch:
        @pl.when((i == 0) & (j == 0))
        def _():
            for ref in epi_scratch:
                ref[...] = jnp.zeros_like(ref)

    if row_subtiles == 1:
        h = h_ref[...]
        ys = [jnp.dot(h, wbf_ref[...], preferred_element_type=F32) for wbf_ref in wbf_refs]
        epilogue(ys, extra_refs, out_refs, epi_scratch, i, j)
    else:
        sub = h_ref.shape[0] // row_subtiles

        def row_tile(r, carry):
            rows = pl.ds(pl.multiple_of(r * sub, sub), sub)
            h = h_ref[rows, :]
            ys = [jnp.dot(h, wbf_ref[...], preferred_element_type=F32) for wbf_ref in wbf_refs]
            epilogue(ys, extra_refs, [o.at[rows, :] for o in out_refs], epi_scratch,
                     i * row_subtiles + r, j)
            return carry

        lax.fori_loop(0, row_subtiles, row_tile, 0)
    for (_, _, fn), ins, outs in zip(side_jobs, side_in_refs, side_out_refs):
        fn(ins, outs)


def _up_proj(h, weights, *, n_col_tiles, bm, bn, weight_stationary, epilogue, extras=(), outs,
             epi_scratch=(), sides=(), row_subtiles=1, name):
    m, d = h.shape
    n_i, n_j = m // bm, n_col_tiles
    cast_weights = weights[0][0].dtype != BF16
    assert weight_stationary or not cast_weights
    if weight_stationary:
        grid = (n_j, n_i)

        def ij(fn):
            return lambda j, i: fn(i, j)

        def step(fn):
            return lambda j, i: fn(j * n_i + i)
    else:
        grid = (n_i, n_j)

        def ij(fn):
            return fn

        def step(fn):
            return lambda i, j: fn(i * n_j + j)

    built = [side(n_i * n_j) for side in sides]
    side_ins = [entry for ins, _, _ in built for entry in ins]
    side_outs = [entry for _, outs_, _ in built for entry in outs_]
    in_specs = [pl.BlockSpec((bm, d), ij(lambda i, j: (i, 0)))]
    for _, layer, c0 in weights:
        in_specs.append(pl.BlockSpec((None, d, bn), ij(lambda i, j, layer=layer, c0=c0: (layer, 0, j + c0))))
    in_specs += [pl.BlockSpec(shape, ij(fn)) for _, shape, fn in extras]
    in_specs += [pl.BlockSpec(shape, step(fn)) for _, shape, fn in side_ins]
    out_specs = [pl.BlockSpec(shape, ij(fn)) for _, shape, fn in outs]
    out_specs += [pl.BlockSpec(shape, step(fn)) for _, shape, fn in side_outs]
    scratch = []
    if cast_weights:
        scratch += [pltpu.VMEM((d, bn), BF16) for _ in weights]
    scratch += list(epi_scratch)
    kern = functools.partial(_up_kernel, n_w=len(weights), n_extra=len(extras), n_out=len(outs),
                             side_jobs=[(len(ins), len(outs_), fn) for ins, outs_, fn in built],
                             weight_stationary=weight_stationary, cast_weights=cast_weights,
                             row_subtiles=row_subtiles, epilogue=epilogue)
    res = pl.pallas_call(
        kern,
        grid=grid,
        in_specs=in_specs,
        out_specs=out_specs,
        out_shape=[sds for sds, _, _ in outs] + [sds for sds, _, _ in side_outs],
        scratch_shapes=scratch,
        compiler_params=_params("arbitrary", "arbitrary"),
        name=name,
    )(h, *[w for w, _, _ in weights], *[a for a, _, _ in extras], *[a for a, _, _ in side_ins])
    return res[:len(outs)], res[len(outs):]


def _cast_rows_side(w, layer, *, n_used=None, keep_layer_axis=False):
    kdim, d = w.shape[1:]

    def build(n_steps):
        used = n_steps if n_used is None else n_used
        rows = kdim // used
        assert used <= n_steps and rows * used == kdim and rows % (2 * SUBLANES) == 0, (kdim, used)

        def fn(in_refs, out_refs):
            out_refs[0][...] = in_refs[0][...].astype(BF16)

        def block(t):
            return jnp.minimum(t, used - 1)

        if keep_layer_axis:
            out = (jax.ShapeDtypeStruct((1, kdim, d), BF16), (None, rows, d), lambda t: (0, block(t), 0))
        else:
            out = (jax.ShapeDtypeStruct((kdim, d), BF16), (rows, d), lambda t: (block(t), 0))
        return [(w, (None, rows, d), lambda t: (layer, block(t), 0))], [out], fn

    return build


def _lane_partial_sum(y):
    acc = y[:, 0:LANES]
    for c in range(1, y.shape[1] // LANES):
        acc = acc + y[:, c * LANES:(c + 1) * LANES]
    return acc


def _gelu_exact(y):
    return 0.5 * y * (1.0 + lax.erf(y * np.float32(np.sqrt(0.5))))


def _gmlp_in_epilogue(ys, extras, outs, scratch, ip, jp, *, n_u_tiles, n_tiles, v_width):
    z_ref, mu_ref, rstd_ref = outs
    s_ref, ss_ref = scratch
    z = _gelu_exact(ys[0])
    z_ref[...] = z.astype(z_ref.dtype)
    carry = jp > n_u_tiles
    in_v = jp >= n_u_tiles
    s_ref[...] = jnp.where(carry, s_ref[...], 0.0) + jnp.where(in_v, _lane_partial_sum(z), 0.0)
    ss_ref[...] = jnp.where(carry, ss_ref[...], 0.0) + jnp.where(in_v, _lane_partial_sum(z * z), 0.0)

    @pl.when(jp == n_tiles - 1)
    def _():
        mu = jnp.sum(s_ref[...], axis=-1, keepdims=True) / v_width
        var = jnp.sum(ss_ref[...], axis=-1, keepdims=True) / v_width - mu * mu
        mu_ref[...] = mu
        rstd_ref[...] = lax.rsqrt(var + EPS)


def _gmlp_in(h, w, w_out, *, bm, bn):
    m = h.shape[0]
    n = w.shape[2]
    n_tiles = n // bn
    epilogue = functools.partial(_gmlp_in_epilogue, n_u_tiles=n_tiles // 2, n_tiles=n_tiles,
                                 v_width=n // 2)
    (z, mu, rstd), (w_out_bf,) = _up_proj(
        h, [(w, 0, 0)], n_col_tiles=n_tiles, bm=bm, bn=bn, weight_stationary=False, epilogue=epilogue,
        outs=[(jax.ShapeDtypeStruct((m, n), BF16), (bm, bn), lambda i, j: (i, j)),
              (jax.ShapeDtypeStruct((m, 1), F32), (bm, 1), lambda i, j: (i, 0)),
              (jax.ShapeDtypeStruct((m, 1), F32), (bm, 1), lambda i, j: (i, 0))],
        epi_scratch=[pltpu.VMEM((bm, LANES), F32), pltpu.VMEM((bm, LANES), F32)],
        sides=[_cast_rows_side(w_out, 0)], name="gmlp_in")
    return z, mu, rstd, w_out_bf


def _causal_conv(z, prev, cw, cb):
    row = lax.broadcasted_iota(jnp.int32, z.shape, 0)
    z1 = jnp.where(row == 0, prev[7:8, :], pltpu.roll(z, 1, 0))
    z2 = jnp.where(row == 0, prev[6:7, :], jnp.where(row == 1, prev[7:8, :], pltpu.roll(z, 2, 0)))
    return cb + cw[0:1, :] * z2 + cw[1:2, :] * z1 + cw[2:3, :] * z


def _ffn_up_epilogue(ys, extras, outs, scratch, ip, jp, *, tiles_per_seq):
    cwg_ref, cwu_ref, cbg_ref, cbu_ref = extras
    seq_start = ip % tiles_per_seq == 0

    def branch(z, cw_ref, cb_ref, carry_ref):
        prev = jnp.where(seq_start, 0.0, carry_ref[...])
        out = _causal_conv(z, prev, cw_ref[...], cb_ref[...])
        carry_ref[...] = z[z.shape[0] - SUBLANES:, :]
        return out

    gate = branch(ys[0], cwg_ref, cbg_ref, scratch[0])
    up = branch(ys[1], cwu_ref, cbu_ref, scratch[1])
    outs[0][...] = (jax.nn.silu(gate) * up).astype(outs[0].dtype)


def _ffn_up(h, w, conv_w, conv_b, w_down, layer, *, seq, bm, tn, row_subtiles, more_sides=()):
    m = h.shape[0]
    f = w.shape[2] // 2
    nj = f // tn
    cb = conv_b.reshape(conv_b.shape[0], 1, 2 * f)
    (a,), side_outs = _up_proj(
        h, [(w, layer, 0), (w, layer, nj)], n_col_tiles=nj, bm=bm, bn=tn, weight_stationary=True,
        epilogue=functools.partial(_ffn_up_epilogue, tiles_per_seq=seq * row_subtiles // bm),
        extras=[(conv_w, (None, CONV_WIDTH, tn), lambda i, j: (layer, 0, j)),
                (conv_w, (None, CONV_WIDTH, tn), lambda i, j: (layer, 0, j + nj)),
                (cb, (None, 1, tn), lambda i, j: (layer, 0, j)),
                (cb, (None, 1, tn), lambda i, j: (layer, 0, j + nj))],
        outs=[(jax.ShapeDtypeStruct((m, f), BF16), (bm, tn), lambda i, j: (i, j))],
        epi_scratch=[pltpu.VMEM((SUBLANES, tn), F32), pltpu.VMEM((SUBLANES, tn), F32)],
        sides=[_cast_rows_side(w_down, layer), *more_sides], row_subtiles=row_subtiles, name="ffn_up")
    return (a, *side_outs)


def _rope_epilogue(ys, extras, outs, scratch, ip, jp, *, n_q_tiles, head_qk):
    cos = extras[0][...]
    sin = extras[1][...]
    y, o_ref = ys[0], outs[0]
    even = (lax.broadcasted_iota(jnp.int32, cos.shape, 1) & 1) == 0
    scale = jnp.where(jp >= n_q_tiles, head_qk ** -0.5, 1.0).astype(F32)
    for hh in range(y.shape[1] // head_qk):
        t = y[:, hh * head_qk:(hh + 1) * head_qk]
        nxt = pltpu.roll(t, head_qk - 1, 1)
        prv = pltpu.roll(t, 1, 1)
        r = (t * cos + jnp.where(even, -nxt, prv) * sin) * scale
        o_ref[hh] = r.astype(o_ref.dtype)


def _value_gate_epilogue(ys, extras, outs, scratch, ip, jp, *, n_v_tiles):
    o_ref = outs[0]
    head_v = o_ref.shape[2]
    y = ys[0]
    y = jnp.where(jp >= n_v_tiles, jax.nn.silu(y), y).astype(o_ref.dtype)
    for hh in range(o_ref.shape[0]):
        o_ref[hh] = y[:, hh * head_v:(hh + 1) * head_v]


def _ret_in(h, w, cos, sin, *, bm, bn, qk_dim, v_dim, head_qk):
    m = h.shape[0]
    nq = qk_dim // bn
    nv = v_dim // bn
    head_v = v_dim // RET_HEADS
    tab = (bm, head_qk)
    qk_heads = bn // head_qk
    vg_heads = bn // head_v
    (qk,), _ = _up_proj(
        h, [(w, 0, 0)], n_col_tiles=2 * nq, bm=bm, bn=bn, weight_stationary=True,
        epilogue=functools.partial(_rope_epilogue, n_q_tiles=nq, head_qk=head_qk),
        extras=[(cos, tab, lambda i, j: (i, 0)), (sin, tab, lambda i, j: (i, 0))],
        outs=[(jax.ShapeDtypeStruct((2 * RET_HEADS, m, head_qk), BF16), (qk_heads, bm, head_qk),
               lambda i, j: (j, i, 0))],
        name="ret_in_qk")
    (vg,), _ = _up_proj(
        h, [(w, 0, 2 * nq)], n_col_tiles=2 * nv, bm=bm, bn=bn, weight_stationary=True,
        epilogue=functools.partial(_value_gate_epilogue, n_v_tiles=nv),
        outs=[(jax.ShapeDtypeStruct((2 * RET_HEADS, m, head_v), BF16), (vg_heads, bm, head_v),
               lambda i, j: (j, i, 0))],
        name="ret_in_vg")
    return qk, vg


def _residual_norm_epilogue(acc, x_ref, pg_ref, ng_ref, xo_ref, ho_ref):
    xn = x_ref[...] + _rms(acc) * pg_ref[...]
    xo_ref[...] = xn
    if ho_ref is not None:
        ho_ref[...] = (_rms(xn) * ng_ref[...]).astype(ho_ref.dtype)


def _down_kernel(a_ref, w_ref, x_ref, pg_ref, *rest, with_next):
    if with_next:
        ng_ref, xo_ref, ho_ref, raw0_ref, raw1_ref = rest
    else:
        (xo_ref, raw0_ref, raw1_ref), ng_ref, ho_ref = rest, None, None
    raws = (raw0_ref, raw1_ref)

    def matmul(par):
        raws[par][...] = jnp.dot(a_ref[...], w_ref[...], preferred_element_type=F32)

    def finish(par):
        _residual_norm_epilogue(raws[par][...], x_ref, pg_ref, ng_ref, xo_ref, ho_ref)

    _staged_pipeline(pl.program_id(0), pl.num_programs(0) - 1, [matmul, finish])


def _down_proj(a, w, x, post_g, next_g=None, *, bm):
    m, kdim = a.shape
    d = w.shape[1]
    n = m // bm
    with_next = next_g is not None
    prev_row = pl.BlockSpec((bm, d), lambda s: (jnp.maximum(s - 1, 0), 0))
    gain = pl.BlockSpec((1, d), lambda s: (0, 0))
    gains = [post_g.reshape(1, d)] + ([next_g.reshape(1, d)] if with_next else [])
    out_shape = [jax.ShapeDtypeStruct((m, d), F32)]
    if with_next:
        out_shape.append(jax.ShapeDtypeStruct((m, d), BF16))
    res = pl.pallas_call(
        functools.partial(_down_kernel, with_next=with_next),
        grid=(n + 1,),
        in_specs=[
            pl.BlockSpec((bm, kdim), lambda s: (jnp.minimum(s, n - 1), 0)),
            pl.BlockSpec((kdim, d), lambda s: (0, 0)),
            prev_row,
        ] + [gain] * len(gains),
        out_specs=[prev_row] * len(out_shape),
        out_shape=out_shape,
        scratch_shapes=[pltpu.VMEM((bm, d), F32), pltpu.VMEM((bm, d), F32)],
        compiler_params=_params("arbitrary"),
        name="down_proj",
    )(a, w, x, *gains)
    return res if with_next else (res[0], None)


def _gmlp_down_kernel(u_ref, v_ref, mu_ref, rstd_ref, lg_ref, lb_ref, ws_ref, bs_ref, w_ref,
                      x_ref, pg_ref, ng_ref, xo_ref, ho_ref, raw0_ref, raw1_ref):
    raws = (raw0_ref, raw1_ref)
    bm, half = u_ref.shape
    gd = half // GMLP_GROUPS

    def gate_and_project(par):
        tril = (lax.broadcasted_iota(jnp.int32, (CHUNK, CHUNK), 0)
                >= lax.broadcasted_iota(jnp.int32, (CHUNK, CHUNK), 1))
        mu = mu_ref[...]
        rstd = rstd_ref[...]
        acc = None
        for g in range(GMLP_GROUPS):
            cols = slice(g * gd, (g + 1) * gd)
            ws = jnp.where(tril, ws_ref[g], 0.0).astype(BF16)
            bias = bs_ref[g]
            vn = (v_ref[:, cols].astype(F32) - mu) * rstd * lg_ref[:, cols] + lb_ref[:, cols]
            vn = vn.astype(BF16)
            gated = []
            for c in range(bm // CHUNK):
                rows = slice(c * CHUNK, (c + 1) * CHUNK)
                mixed = jnp.dot(ws, vn[rows], preferred_element_type=F32) + bias
                gated.append((u_ref[rows, cols].astype(F32) * mixed).astype(BF16))
            part = jnp.dot(jnp.concatenate(gated, axis=0), w_ref[cols, :], preferred_element_type=F32)
            acc = part if acc is None else acc + part
        raws[par][...] = acc

    def finish(par):
        _residual_norm_epilogue(raws[par][...], x_ref, pg_ref, ng_ref, xo_ref, ho_ref)

    _staged_pipeline(pl.program_id(0), pl.num_programs(0) - 1, [gate_and_project, finish])


def _gmlp_down(z, mu, rstd, ln_g, ln_b, w_s, b_s, w, x, post_g, next_g, *, bm):
    m = z.shape[0]
    half, d = w.shape
    n = m // bm

    def first(s):
        return jnp.minimum(s, n - 1)

    def last(s):
        return jnp.maximum(s - 1, 0)

    row = pl.BlockSpec((bm, d), lambda s: (last(s), 0))
    gain = pl.BlockSpec((1, d), lambda s: (0, 0))
    stat = pl.BlockSpec((bm, 1), lambda s: (first(s), 0))
    lnp = pl.BlockSpec((1, half), lambda s: (0, 0))
    return pl.pallas_call(
        _gmlp_down_kernel,
        grid=(n + 1,),
        in_specs=[
            pl.BlockSpec((bm, half), lambda s: (first(s), 0)),
            pl.BlockSpec((bm, half), lambda s: (first(s), 1)),
            stat, stat, lnp, lnp,
            pl.BlockSpec((GMLP_GROUPS, CHUNK, CHUNK), lambda s: (0, 0, 0)),
            pl.BlockSpec((GMLP_GROUPS, CHUNK, 1), lambda s: (0, 0, 0)),
            pl.BlockSpec((half, d), lambda s: (0, 0)),
            row, gain, gain,
        ],
        out_specs=[row, row],
        out_shape=[jax.ShapeDtypeStruct((m, d), F32), jax.ShapeDtypeStruct((m, d), BF16)],
        scratch_shapes=[pltpu.VMEM((bm, d), F32), pltpu.VMEM((bm, d), F32)],
        compiler_params=_params("arbitrary"),
        name="gmlp_down",
    )(z, z, mu, rstd, ln_g.reshape(1, half), ln_b.reshape(1, half), w_s,
      b_s.reshape(GMLP_GROUPS, CHUNK, 1), w, x, post_g.reshape(1, d), next_g.reshape(1, d))


def _rope_table_kernel(pos_ref, inv_ref, cos_ref, sin_ref):
    ang = pos_ref[...].astype(F32) * inv_ref[...]
    cos_ref[...] = jnp.cos(ang)
    sin_ref[...] = jnp.sin(ang)


def _rope_tables(pos, inv_freq, *, bm):
    m = pos.shape[0]
    dk = inv_freq.shape[1]
    tab = pl.BlockSpec((bm, dk), lambda i: (i, 0))
    return pl.pallas_call(
        _rope_table_kernel,
        grid=(m // bm,),
        in_specs=[pl.BlockSpec((bm, 1), lambda i: (i, 0)), pl.BlockSpec((1, dk), lambda i: (0, 0))],
        out_specs=[tab, tab],
        out_shape=[jax.ShapeDtypeStruct((m, dk), F32)] * 2,
        compiler_params=_params("arbitrary"),
        name="rope_tables",
    )(pos, inv_freq)


def _ret_core_kernel(q_ref, k_ref, v_ref, g_ref, dm_ref, qd_ref, kd_ref, cd_ref, wsrc_ref,
                     o_ref, wdst_ref, state_ref):
    t = pl.program_id(2)

    @pl.when(t == 0)
    def _():
        state_ref[...] = jnp.zeros_like(state_ref)

    wdst_ref[...] = wsrc_ref[...].astype(BF16)
    dmask = dm_ref[0]
    qdec = qd_ref[0]
    kdec = kd_ref[0]
    cdec = cd_ref[0]
    chunk = dmask.shape[0]
    state = state_ref[...]
    for c in range(q_ref.shape[0] // chunk):
        rows = slice(c * chunk, (c + 1) * chunk)
        qc = q_ref[rows, :]
        kc = k_ref[rows, :]
        vc = v_ref[rows, :]
        scores = lax.dot_general(qc, kc, (((1,), (1,)), ((), ())), preferred_element_type=F32) * dmask
        intra = jnp.dot(scores.astype(BF16), vc, preferred_element_type=F32)
        qd = (qc.astype(F32) * qdec).astype(BF16)
        cross = jnp.dot(qd, state.astype(BF16), preferred_element_type=F32)
        kd = (kc.astype(F32) * kdec).astype(BF16)
        upd = lax.dot_general(kd, vc, (((0,), (0,)), ((), ())), preferred_element_type=F32)
        state = state * cdec + upd
        o = _rms(intra + cross)
        o_ref[rows, :] = (g_ref[rows, :].astype(F32) * o).astype(o_ref.dtype)
    state_ref[...] = state


def _ret_core(qk, vg, dmask, qdec, kdec, cdec, w_out, *, batch, seq, ts):
    m, hq = qk.shape[1:]
    hv = vg.shape[2]
    chunk = dmask.shape[1]
    nts = seq // ts
    kdim, d = w_out.shape[1:]
    n_steps = batch * RET_HEADS * nts
    wrows = kdim // n_steps
    assert wrows * n_steps == kdim and wrows % (2 * SUBLANES) == 0, (kdim, n_steps)

    def step(b, h, t):
        return (b * RET_HEADS + h) * nts + t

    return pl.pallas_call(
        _ret_core_kernel,
        grid=(batch, RET_HEADS, nts),
        in_specs=[
            pl.BlockSpec((None, ts, hq), lambda b, h, t: (h, b * nts + t, 0)),
            pl.BlockSpec((None, ts, hq), lambda b, h, t: (RET_HEADS + h, b * nts + t, 0)),
            pl.BlockSpec((None, ts, hv), lambda b, h, t: (h, b * nts + t, 0)),
            pl.BlockSpec((None, ts, hv), lambda b, h, t: (RET_HEADS + h, b * nts + t, 0)),
            pl.BlockSpec((1, chunk, chunk), lambda b, h, t: (h, 0, 0)),
            pl.BlockSpec((1, chunk, 1), lambda b, h, t: (h, 0, 0)),
            pl.BlockSpec((1, chunk, 1), lambda b, h, t: (h, 0, 0)),
            pl.BlockSpec((1, 1, 1), lambda b, h, t: (h, 0, 0)),
            pl.BlockSpec((None, wrows, d), lambda b, h, t: (0, step(b, h, t), 0)),
        ],
        out_specs=[pl.BlockSpec((ts, hv), lambda b, h, t: (b * nts + t, h)),
                   pl.BlockSpec((wrows, d), lambda b, h, t: (step(b, h, t), 0))],
        out_shape=[jax.ShapeDtypeStruct((m, hv * RET_HEADS), BF16),
                   jax.ShapeDtypeStruct((kdim, d), BF16)],
        scratch_shapes=[pltpu.VMEM((hq, hv), F32)],
        compiler_params=_params("arbitrary", "arbitrary", "arbitrary"),
        name="ret_core",
    )(qk, qk, vg, vg, dmask, qdec, kdec, cdec, w_out)


def _retention_constants(head_qk, chunk):
    log_gamma = jnp.log1p(-jnp.exp2(-5.0 - jnp.arange(RET_HEADS, dtype=F32)))
    idx = jnp.arange(chunk, dtype=F32)
    rel = idx[:, None] - idx[None, :]
    dmask = jnp.where(rel[None] >= 0,
                      jnp.exp(jnp.maximum(rel, 0.0)[None] * log_gamma[:, None, None]), 0.0)
    qdec = jnp.exp((idx[None, :] + 1.0) * log_gamma[:, None])[:, :, None]
    kdec = jnp.exp((chunk - 1.0 - idx)[None, :] * log_gamma[:, None])[:, :, None]
    cdec = jnp.exp(chunk * log_gamma)[:, None, None]
    inv_freq = 1.0 / (ROPE_BASE ** jnp.linspace(0.0, 1.0, head_qk // 2, dtype=F32))
    inv_freq = jnp.repeat(inv_freq, 2)[None, :]
    return dmask, qdec, kdec, cdec, inv_freq


def kernel(x, positions, mix_pre_g, mix_post_g, gmlp_w_in, gmlp_ln_g, gmlp_ln_b, gmlp_w_s, gmlp_b_s,
           gmlp_w_out, ret_w_in, ret_w_out, ffn_pre_g, ffn_post_g, ffn_w_up, ffn_conv_w, ffn_conv_b,
           ffn_w_down):
    batch, seq, d = x.shape
    m = batch * seq
    qk_dim = d
    v_dim = (ret_w_in.shape[2] - 2 * qk_dim) // 2
    head_qk = qk_dim // RET_HEADS
    xf = x.reshape(m, d)

    h = _prenorm(xf, mix_pre_g[0], bm=512)
    z, mu, rstd, w_bf = _gmlp_in(h, gmlp_w_in.astype(BF16), gmlp_w_out, bm=1024, bn=2048)
    xf, h = _gmlp_down(z, mu, rstd, gmlp_ln_g[0], gmlp_ln_b[0], gmlp_w_s[0], gmlp_b_s[0],
                       w_bf, xf, mix_post_g[0], ffn_pre_g[0], bm=256)
    ret_w_in_side = _cast_rows_side(ret_w_in, 0, n_used=64, keep_layer_axis=True)
    a, w_bf, ret_w_in_bf = _ffn_up(h, ffn_w_up, ffn_conv_w, ffn_conv_b, ffn_w_down, 0, seq=seq, bm=2048,
                                   tn=512, row_subtiles=2, more_sides=[ret_w_in_side])
    xf, h = _down_proj(a, w_bf, xf, ffn_post_g[0], mix_pre_g[1], bm=256)

    dmask, qdec, kdec, cdec, inv_freq = _retention_constants(head_qk, RET_CHUNK)
    cos, sin = _rope_tables(positions.reshape(m, 1), inv_freq, bm=1024)
    qk, vg = _ret_in(h, ret_w_in_bf, cos, sin, bm=1024, bn=2048,
                     qk_dim=qk_dim, v_dim=v_dim, head_qk=head_qk)
    o, w_bf = _ret_core(qk, vg, dmask, qdec, kdec, cdec, ret_w_out, batch=batch, seq=seq, ts=4096)
    xf, h = _down_proj(o, w_bf, xf, mix_post_g[1], ffn_pre_g[1], bm=256)
    a, w_bf = _ffn_up(h, ffn_w_up, ffn_conv_w, ffn_conv_b, ffn_w_down, 1, seq=seq, bm=2048, tn=512,
                      row_subtiles=2)
    xf, _ = _down_proj(a, w_bf, xf, ffn_post_g[1], bm=256)
    return xf.reshape(batch, seq, d)
```

```python
import functools

import jax
import jax.numpy as jnp
import numpy as np
from jax import lax
from jax.experimental import pallas as pl
from jax.experimental.pallas import tpu as pltpu

EPS = 1e-6
CHUNK = 128
RET_CHUNK = 256
GMLP_GROUPS = 8
RET_HEADS = 8
ROPE_BASE = 10000.0
CONV_WIDTH = 3

F32 = jnp.float32
BF16 = jnp.bfloat16

V7X_VMEM_BYTES = 64 * 1024 * 1024
VMEM_LIMIT_BYTES = V7X_VMEM_BYTES - 8 * 1024 * 1024
LANES = 128
SUBLANES = 8


def _params(*semantics):
    return pltpu.CompilerParams(dimension_semantics=semantics, vmem_limit_bytes=VMEM_LIMIT_BYTES)


def _rms(x):
    return x * lax.rsqrt(jnp.mean(x * x, axis=-1, keepdims=True) + EPS)


def _staged_pipeline(s, n, stages):
    depth = len(stages)

    def run(active, step_parity):
        for k in active:
            stages[k]((step_parity - k) % 2)

    for fill in range(depth - 1):
        pl.when(s == fill)(functools.partial(run, range(fill + 1), fill % 2))
    for parity in (0, 1):
        pl.when((s >= depth - 1) & (s < n) & (s % 2 == parity))(
            functools.partial(run, range(depth), parity))
    for drain in range(1, depth):
        step = n - 1 + drain
        pl.when(s == step)(functools.partial(run, range(drain, depth), step % 2))


def _up_kernel(*refs, n_w, n_extra, n_out, side_jobs, weight_stationary, cast_weights, row_subtiles,
               prenorm, epilogue):
    h_ref = refs[0]
    if prenorm:
        gain_ref, refs = refs[1], refs[:1] + refs[2:]
    w_refs = refs[1:1 + n_w]
    pos = 1 + n_w
    extra_refs = refs[pos:pos + n_extra]
    pos += n_extra
    side_in_refs = []
    for n_in, _, _ in side_jobs:
        side_in_refs.append(refs[pos:pos + n_in])
        pos += n_in
    out_refs = refs[pos:pos + n_out]
    pos += n_out
    side_out_refs = []
    for _, n_o, _ in side_jobs:
        side_out_refs.append(refs[pos:pos + n_o])
        pos += n_o
    if cast_weights:
        wbf_refs = refs[pos:pos + n_w]
        pos += n_w
    else:
        wbf_refs = w_refs
    if prenorm:
        hn_ref = refs[pos]
        pos += 1
    epi_scratch = refs[pos:]

    if weight_stationary:
        j, i = pl.program_id(0), pl.program_id(1)
    else:
        i, j = pl.program_id(0), pl.program_id(1)

    if prenorm:
        @pl.when(j == 0)
        def _():
            hn_ref[...] = (_rms(h_ref[...]) * gain_ref[...]).astype(BF16)

        h_ref = hn_ref

    if cast_weights:
        @pl.when(i == 0)
        def _():
            for w_ref, wbf_ref in zip(w_refs, wbf_refs):
                wbf_ref[...] = w_ref[...].astype(BF16)

    if epi_scratch:
        @pl.when((i == 0) & (j == 0))
        def _():
            for ref in epi_scratch:
                ref[...] = jnp.zeros_like(ref)

    if row_subtiles == 1:
        h = h_ref[...]
        ys = [jnp.dot(h, wbf_ref[...], preferred_element_type=F32) for wbf_ref in wbf_refs]
        epilogue(ys, extra_refs, out_refs, epi_scratch, i, j)
    else:
        sub = h_ref.shape[0] // row_subtiles

        def row_tile(r, carry):
            rows = pl.ds(pl.multiple_of(r * sub, sub), sub)
            h = h_ref[rows, :]
            ys = [jnp.dot(h, wbf_ref[...], preferred_element_type=F32) for wbf_ref in wbf_refs]
            epilogue(ys, extra_refs, [o.at[rows, :] for o in out_refs], epi_scratch,
                     i * row_subtiles + r, j)
            return carry

        lax.fori_loop(0, row_subtiles, row_tile, 0)
    for (_, _, fn), ins, outs in zip(side_jobs, side_in_refs, side_out_refs):
        fn(ins, outs)


def _up_proj(h, weights, *, n_col_tiles, bm, bn, weight_stationary, epilogue, extras=(), outs,
             epi_scratch=(), sides=(), row_subtiles=1, prenorm_gain=None, name):
    m, d = h.shape
    n_i, n_j = m // bm, n_col_tiles
    cast_weights = weights[0][0].dtype != BF16
    prenorm = prenorm_gain is not None
    assert weight_stationary or not cast_weights
    assert not (prenorm and weight_stationary)
    if weight_stationary:
        grid = (n_j, n_i)

        def ij(fn):
            return lambda j, i: fn(i, j)

        def step(fn):
            return lambda j, i: fn(j * n_i + i)
    else:
        grid = (n_i, n_j)

        def ij(fn):
            return fn

        def step(fn):
            return lambda i, j: fn(i * n_j + j)

    built = [side(n_i * n_j) for side in sides]
    side_ins = [entry for ins, _, _ in built for entry in ins]
    side_outs = [entry for _, outs_, _ in built for entry in outs_]
    in_specs = [pl.BlockSpec((bm, d), ij(lambda i, j: (i, 0)))]
    if prenorm:
        in_specs.append(pl.BlockSpec((1, d), ij(lambda i, j: (0, 0))))
    for _, layer, c0 in weights:
        in_specs.append(pl.BlockSpec((None, d, bn), ij(lambda i, j, layer=layer, c0=c0: (layer, 0, j + c0))))
    in_specs += [pl.BlockSpec(shape, ij(fn)) for _, shape, fn in extras]
    in_specs += [pl.BlockSpec(shape, step(fn)) for _, shape, fn in side_ins]
    out_specs = [pl.BlockSpec(shape, ij(fn)) for _, shape, fn in outs]
    out_specs += [pl.BlockSpec(shape, step(fn)) for _, shape, fn in side_outs]
    scratch = []
    if cast_weights:
        scratch += [pltpu.VMEM((d, bn), BF16) for _ in weights]
    if prenorm:
        scratch.append(pltpu.VMEM((bm, d), BF16))
    scratch += list(epi_scratch)
    kern = functools.partial(_up_kernel, n_w=len(weights), n_extra=len(extras), n_out=len(outs),
                             side_jobs=[(len(ins), len(outs_), fn) for ins, outs_, fn in built],
                             weight_stationary=weight_stationary, cast_weights=cast_weights,
                             row_subtiles=row_subtiles, prenorm=prenorm, epilogue=epilogue)
    res = pl.pallas_call(
        kern,
        grid=grid,
        in_specs=in_specs,
        out_specs=out_specs,
        out_shape=[sds for sds, _, _ in outs] + [sds for sds, _, _ in side_outs],
        scratch_shapes=scratch,
        compiler_params=_params("arbitrary", "arbitrary"),
        name=name,
    )(h, *([prenorm_gain.reshape(1, d)] if prenorm else []), *[w for w, _, _ in weights],
      *[a for a, _, _ in extras], *[a for a, _, _ in side_ins])
    return res[:len(outs)], res[len(outs):]


def _cast_rows_side(w, layer, *, n_used=None, keep_layer_axis=False):
    kdim, d = w.shape[1:]

    def build(n_steps):
        used = n_steps if n_used is None else n_used
        rows = kdim // used
        assert used <= n_steps and rows * used == kdim and rows % (2 * SUBLANES) == 0, (kdim, used)

        def fn(in_refs, out_refs):
            out_refs[0][...] = in_refs[0][...].astype(BF16)

        def block(t):
            return jnp.minimum(t, used - 1)

        if keep_layer_axis:
            out = (jax.ShapeDtypeStruct((1, kdim, d), BF16), (None, rows, d), lambda t: (0, block(t), 0))
        else:
            out = (jax.ShapeDtypeStruct((kdim, d), BF16), (rows, d), lambda t: (block(t), 0))
        return [(w, (None, rows, d), lambda t: (layer, block(t), 0))], [out], fn

    return build


def _lane_partial_sum(y):
    acc = y[:, 0:LANES]
    for c in range(1, y.shape[1] // LANES):
        acc = acc + y[:, c * LANES:(c + 1) * LANES]
    return acc


def _gelu_exact(y):
    return 0.5 * y * (1.0 + lax.erf(y * np.float32(np.sqrt(0.5))))


def _gmlp_in_epilogue(ys, extras, outs, scratch, ip, jp, *, n_u_tiles, n_tiles, v_width):
    z_ref, mu_ref, rstd_ref = outs
    s_ref, ss_ref = scratch
    z = _gelu_exact(ys[0])
    z_ref[...] = z.astype(z_ref.dtype)
    carry = jp > n_u_tiles
    in_v = jp >= n_u_tiles
    s_ref[...] = jnp.where(carry, s_ref[...], 0.0) + jnp.where(in_v, _lane_partial_sum(z), 0.0)
    ss_ref[...] = jnp.where(carry, ss_ref[...], 0.0) + jnp.where(in_v, _lane_partial_sum(z * z), 0.0)

    @pl.when(jp == n_tiles - 1)
    def _():
        mu = jnp.sum(s_ref[...], axis=-1, keepdims=True) / v_width
        var = jnp.sum(ss_ref[...], axis=-1, keepdims=True) / v_width - mu * mu
        mu_ref[...] = mu
        rstd_ref[...] = lax.rsqrt(var + EPS)


def _gmlp_in(x, pre_g, w, w_out, *, bm, bn):
    h = x
    m = h.shape[0]
    n = w.shape[2]
    n_tiles = n // bn
    epilogue = functools.partial(_gmlp_in_epilogue, n_u_tiles=n_tiles // 2, n_tiles=n_tiles,
                                 v_width=n // 2)
    (z, mu, rstd), (w_out_bf,) = _up_proj(
        h, [(w, 0, 0)], n_col_tiles=n_tiles, bm=bm, bn=bn, weight_stationary=False, epilogue=epilogue,
        outs=[(jax.ShapeDtypeStruct((m, n), BF16), (bm, bn), lambda i, j: (i, j)),
              (jax.ShapeDtypeStruct((m, 1), F32), (bm, 1), lambda i, j: (i, 0)),
              (jax.ShapeDtypeStruct((m, 1), F32), (bm, 1), lambda i, j: (i, 0))],
        epi_scratch=[pltpu.VMEM((bm, LANES), F32), pltpu.VMEM((bm, LANES), F32)],
        sides=[_cast_rows_side(w_out, 0)], prenorm_gain=pre_g, name="gmlp_in")
    return z, mu, rstd, w_out_bf


def _causal_conv(z, prev, cw, cb):
    row = lax.broadcasted_iota(jnp.int32, z.shape, 0)
    z1 = jnp.where(row == 0, prev[7:8, :], pltpu.roll(z, 1, 0))
    z2 = jnp.where(row == 0, prev[6:7, :], jnp.where(row == 1, prev[7:8, :], pltpu.roll(z, 2, 0)))
    return cb + cw[0:1, :] * z2 + cw[1:2, :] * z1 + cw[2:3, :] * z


def _ffn_up_epilogue(ys, extras, outs, scratch, ip, jp, *, tiles_per_seq):
    cwg_ref, cwu_ref, cbg_ref, cbu_ref = extras
    seq_start = ip % tiles_per_seq == 0

    def branch(z, cw_ref, cb_ref, carry_ref):
        prev = jnp.where(seq_start, 0.0, carry_ref[...])
        out = _causal_conv(z, prev, cw_ref[...], cb_ref[...])
        carry_ref[...] = z[z.shape[0] - SUBLANES:, :]
        return out

    gate = branch(ys[0], cwg_ref, cbg_ref, scratch[0])
    up = branch(ys[1], cwu_ref, cbu_ref, scratch[1])
    outs[0][...] = (jax.nn.silu(gate) * up).astype(outs[0].dtype)


def _ffn_up(h, w, conv_w, conv_b, w_down, layer, *, seq, bm, tn, row_subtiles, more_sides=()):
    m = h.shape[0]
    f = w.shape[2] // 2
    nj = f // tn
    cb = conv_b.reshape(conv_b.shape[0], 1, 2 * f)
    (a,), side_outs = _up_proj(
        h, [(w, layer, 0), (w, layer, nj)], n_col_tiles=nj, bm=bm, bn=tn, weight_stationary=True,
        epilogue=functools.partial(_ffn_up_epilogue, tiles_per_seq=seq * row_subtiles // bm),
        extras=[(conv_w, (None, CONV_WIDTH, tn), lambda i, j: (layer, 0, j)),
                (conv_w, (None, CONV_WIDTH, tn), lambda i, j: (layer, 0, j + nj)),
                (cb, (None, 1, tn), lambda i, j: (layer, 0, j)),
                (cb, (None, 1, tn), lambda i, j: (layer, 0, j + nj))],
        outs=[(jax.ShapeDtypeStruct((m, f), BF16), (bm, tn), lambda i, j: (i, j))],
        epi_scratch=[pltpu.VMEM((SUBLANES, tn), F32), pltpu.VMEM((SUBLANES, tn), F32)],
        sides=[_cast_rows_side(w_down, layer), *more_sides], row_subtiles=row_subtiles, name="ffn_up")
    return (a, *side_outs)


def _rope_epilogue(ys, extras, outs, scratch, ip, jp, *, n_q_tiles, head_qk):
    cos = extras[0][...]
    sin = extras[1][...]
    y, o_ref = ys[0], outs[0]
    even = (lax.broadcasted_iota(jnp.int32, cos.shape, 1) & 1) == 0
    scale = jnp.where(jp >= n_q_tiles, head_qk ** -0.5, 1.0).astype(F32)
    for hh in range(y.shape[1] // head_qk):
        t = y[:, hh * head_qk:(hh + 1) * head_qk]
        nxt = pltpu.roll(t, head_qk - 1, 1)
        prv = pltpu.roll(t, 1, 1)
        r = (t * cos + jnp.where(even, -nxt, prv) * sin) * scale
        o_ref[hh] = r.astype(o_ref.dtype)


def _value_gate_epilogue(ys, extras, outs, scratch, ip, jp, *, n_v_tiles):
    o_ref = outs[0]
    head_v = o_ref.shape[2]
    y = ys[0]
    y = jnp.where(jp >= n_v_tiles, jax.nn.silu(y), y).astype(o_ref.dtype)
    for hh in range(o_ref.shape[0]):
        o_ref[hh] = y[:, hh * head_v:(hh + 1) * head_v]


def _ret_in(h, w, cos, sin, *, bm, bn, qk_dim, v_dim, head_qk):
    m = h.shape[0]
    nq = qk_dim // bn
    nv = v_dim // bn
    head_v = v_dim // RET_HEADS
    tab = (bm, head_qk)
    qk_heads = bn // head_qk
    vg_heads = bn // head_v
    (qk,), _ = _up_proj(
        h, [(w, 0, 0)], n_col_tiles=2 * nq, bm=bm, bn=bn, weight_stationary=True,
        epilogue=functools.partial(_rope_epilogue, n_q_tiles=nq, head_qk=head_qk),
        extras=[(cos, tab, lambda i, j: (i, 0)), (sin, tab, lambda i, j: (i, 0))],
        outs=[(jax.ShapeDtypeStruct((2 * RET_HEADS, m, head_qk), BF16), (qk_heads, bm, head_qk),
               lambda i, j: (j, i, 0))],
        name="ret_in_qk")
    (vg,), _ = _up_proj(
        h, [(w, 0, 2 * nq)], n_col_tiles=2 * nv, bm=bm, bn=bn, weight_stationary=True,
        epilogue=functools.partial(_value_gate_epilogue, n_v_tiles=nv),
        outs=[(jax.ShapeDtypeStruct((2 * RET_HEADS, m, head_v), BF16), (vg_heads, bm, head_v),
               lambda i, j: (j, i, 0))],
        name="ret_in_vg")
    return qk, vg


def _residual_norm_epilogue(acc, x_ref, pg_ref, ng_ref, xo_ref, ho_ref):
    xn = x_ref[...] + _rms(acc) * pg_ref[...]
    xo_ref[...] = xn
    if ho_ref is not None:
        ho_ref[...] = (_rms(xn) * ng_ref[...]).astype(ho_ref.dtype)


def _down_kernel(a_ref, w_ref, x_ref, pg_ref, *rest, with_next):
    if with_next:
        ng_ref, xo_ref, ho_ref, raw0_ref, raw1_ref = rest
    else:
        (xo_ref, raw0_ref, raw1_ref), ng_ref, ho_ref = rest, None, None
    raws = (raw0_ref, raw1_ref)

    def matmul(par):
        raws[par][...] = jnp.dot(a_ref[...], w_ref[...], preferred_element_type=F32)

    def finish(par):
        _residual_norm_epilogue(raws[par][...], x_ref, pg_ref, ng_ref, xo_ref, ho_ref)

    _staged_pipeline(pl.program_id(0), pl.num_programs(0) - 1, [matmul, finish])


def _down_proj(a, w, x, post_g, next_g=None, *, bm):
    m, kdim = a.shape
    d = w.shape[1]
    n = m // bm
    with_next = next_g is not None
    prev_row = pl.BlockSpec((bm, d), lambda s: (jnp.maximum(s - 1, 0), 0))
    gain = pl.BlockSpec((1, d), lambda s: (0, 0))
    gains = [post_g.reshape(1, d)] + ([next_g.reshape(1, d)] if with_next else [])
    out_shape = [jax.ShapeDtypeStruct((m, d), F32)]
    if with_next:
        out_shape.append(jax.ShapeDtypeStruct((m, d), BF16))
    res = pl.pallas_call(
        functools.partial(_down_kernel, with_next=with_next),
        grid=(n + 1,),
        in_specs=[
            pl.BlockSpec((bm, kdim), lambda s: (jnp.minimum(s, n - 1), 0)),
            pl.BlockSpec((kdim, d), lambda s: (0, 0)),
            prev_row,
        ] + [gain] * len(gains),
        out_specs=[prev_row] * len(out_shape),
        out_shape=out_shape,
        scratch_shapes=[pltpu.VMEM((bm, d), F32), pltpu.VMEM((bm, d), F32)],
        compiler_params=_params("arbitrary"),
        name="down_proj",
    )(a, w, x, *gains)
    return res if with_next else (res[0], None)


def _gmlp_down_kernel(u_ref, v_ref, mu_ref, rstd_ref, lg_ref, lb_ref, ws_ref, bs_ref, w_ref,
                      x_ref, pg_ref, ng_ref, xo_ref, ho_ref, raw0_ref, raw1_ref):
    raws = (raw0_ref, raw1_ref)
    bm, half = u_ref.shape
    gd = half // GMLP_GROUPS

    def gate_and_project(par):
        tril = (lax.broadcasted_iota(jnp.int32, (CHUNK, CHUNK), 0)
                >= lax.broadcasted_iota(jnp.int32, (CHUNK, CHUNK), 1))
        mu = mu_ref[...]
        rstd = rstd_ref[...]
        acc = None
        for g in range(GMLP_GROUPS):
            cols = slice(g * gd, (g + 1) * gd)
            ws = jnp.where(tril, ws_ref[g], 0.0).astype(BF16)
            bias = bs_ref[g]
            vn = (v_ref[:, cols].astype(F32) - mu) * rstd * lg_ref[:, cols] + lb_ref[:, cols]
            vn = vn.astype(BF16)
            gated = []
            for c in range(bm // CHUNK):
                rows = slice(c * CHUNK, (c + 1) * CHUNK)
                mixed = jnp.dot(ws, vn[rows], preferred_element_type=F32) + bias
                gated.append((u_ref[rows, cols].astype(F32) * mixed).astype(BF16))
            part = jnp.dot(jnp.concatenate(gated, axis=0), w_ref[cols, :], preferred_element_type=F32)
            acc = part if acc is None else acc + part
        raws[par][...] = acc

    def finish(par):
        _residual_norm_epilogue(raws[par][...], x_ref, pg_ref, ng_ref, xo_ref, ho_ref)

    _staged_pipeline(pl.program_id(0), pl.num_programs(0) - 1, [gate_and_project, finish])


def _gmlp_down(z, mu, rstd, ln_g, ln_b, w_s, b_s, w, x, post_g, next_g, *, bm):
    m = z.shape[0]
    half, d = w.shape
    n = m // bm

    def first(s):
        return jnp.minimum(s, n - 1)

    def last(s):
        return jnp.maximum(s - 1, 0)

    row = pl.BlockSpec((bm, d), lambda s: (last(s), 0))
    gain = pl.BlockSpec((1, d), lambda s: (0, 0))
    stat = pl.BlockSpec((bm, 1), lambda s: (first(s), 0))
    lnp = pl.BlockSpec((1, half), lambda s: (0, 0))
    return pl.pallas_call(
        _gmlp_down_kernel,
        grid=(n + 1,),
        in_specs=[
            pl.BlockSpec((bm, half), lambda s: (first(s), 0)),
            pl.BlockSpec((bm, half), lambda s: (first(s), 1)),
            stat, stat, lnp, lnp,
            pl.BlockSpec((GMLP_GROUPS, CHUNK, CHUNK), lambda s: (0, 0, 0)),
            pl.BlockSpec((GMLP_GROUPS, CHUNK, 1), lambda s: (0, 0, 0)),
            pl.BlockSpec((half, d), lambda s: (0, 0)),
            row, gain, gain,
        ],
        out_specs=[row, row],
        out_shape=[jax.ShapeDtypeStruct((m, d), F32), jax.ShapeDtypeStruct((m, d), BF16)],
        scratch_shapes=[pltpu.VMEM((bm, d), F32), pltpu.VMEM((bm, d), F32)],
        compiler_params=_params("arbitrary"),
        name="gmlp_down",
    )(z, z, mu, rstd, ln_g.reshape(1, half), ln_b.reshape(1, half), w_s,
      b_s.reshape(GMLP_GROUPS, CHUNK, 1), w, x, post_g.reshape(1, d), next_g.reshape(1, d))


def _rope_table_kernel(pos_ref, inv_ref, cos_ref, sin_ref):
    ang = pos_ref[...].astype(F32) * inv_ref[...]
    cos_ref[...] = jnp.cos(ang)
    sin_ref[...] = jnp.sin(ang)


def _rope_tables(pos, inv_freq, *, bm):
    m = pos.shape[0]
    dk = inv_freq.shape[1]
    tab = pl.BlockSpec((bm, dk), lambda i: (i, 0))
    return pl.pallas_call(
        _rope_table_kernel,
        grid=(m // bm,),
        in_specs=[pl.BlockSpec((bm, 1), lambda i: (i, 0)), pl.BlockSpec((1, dk), lambda i: (0, 0))],
        out_specs=[tab, tab],
        out_shape=[jax.ShapeDtypeStruct((m, dk), F32)] * 2,
        compiler_params=_params("arbitrary"),
        name="rope_tables",
    )(pos, inv_freq)


def _ret_core_kernel(q_ref, k_ref, v_ref, g_ref, dm_ref, qd_ref, kd_ref, cd_ref, wsrc_ref,
                     o_ref, wdst_ref, state_ref):
    t = pl.program_id(2)

    @pl.when(t == 0)
    def _():
        state_ref[...] = jnp.zeros_like(state_ref)

    wdst_ref[...] = wsrc_ref[...].astype(BF16)
    dmask = dm_ref[0]
    qdec = qd_ref[0]
    kdec = kd_ref[0]
    cdec = cd_ref[0]
    chunk = dmask.shape[0]
    state = state_ref[...]
    for c in range(q_ref.shape[0] // chunk):
        rows = slice(c * chunk, (c + 1) * chunk)
        qc = q_ref[rows, :]
        kc = k_ref[rows, :]
        vc = v_ref[rows, :]
        scores = lax.dot_general(qc, kc, (((1,), (1,)), ((), ())), preferred_element_type=F32) * dmask
        intra = jnp.dot(scores.astype(BF16), vc, preferred_element_type=F32)
        qd = (qc.astype(F32) * qdec).astype(BF16)
        cross = jnp.dot(qd, state.astype(BF16), preferred_element_type=F32)
        kd = (kc.astype(F32) * kdec).astype(BF16)
        upd = lax.dot_general(kd, vc, (((0,), (0,)), ((), ())), preferred_element_type=F32)
        state = state * cdec + upd
        o = _rms(intra + cross)
        o_ref[rows, :] = (g_ref[rows, :].astype(F32) * o).astype(o_ref.dtype)
    state_ref[...] = state


def _ret_core(qk, vg, dmask, qdec, kdec, cdec, w_out, *, batch, seq, ts):
    m, hq = qk.shape[1:]
    hv = vg.shape[2]
    chunk = dmask.shape[1]
    nts = seq // ts
    kdim, d = w_out.shape[1:]
    n_steps = batch * RET_HEADS * nts
    wrows = kdim // n_steps
    assert wrows * n_steps == kdim and wrows % (2 * SUBLANES) == 0, (kdim, n_steps)

    def step(b, h, t):
        return (b * RET_HEADS + h) * nts + t

    return pl.pallas_call(
        _ret_core_kernel,
        grid=(batch, RET_HEADS, nts),
        in_specs=[
            pl.BlockSpec((None, ts, hq), lambda b, h, t: (h, b * nts + t, 0)),
            pl.BlockSpec((None, ts, hq), lambda b, h, t: (RET_HEADS + h, b * nts + t, 0)),
            pl.BlockSpec((None, ts, hv), lambda b, h, t: (h, b * nts + t, 0)),
            pl.BlockSpec((None, ts, hv), lambda b, h, t: (RET_HEADS + h, b * nts + t, 0)),
            pl.BlockSpec((1, chunk, chunk), lambda b, h, t: (h, 0, 0)),
            pl.BlockSpec((1, chunk, 1), lambda b, h, t: (h, 0, 0)),
            pl.BlockSpec((1, chunk, 1), lambda b, h, t: (h, 0, 0)),
            pl.BlockSpec((1, 1, 1), lambda b, h, t: (h, 0, 0)),
            pl.BlockSpec((None, wrows, d), lambda b, h, t: (0, step(b, h, t), 0)),
        ],
        out_specs=[pl.BlockSpec((ts, hv), lambda b, h, t: (b * nts + t, h)),
                   pl.BlockSpec((wrows, d), lambda b, h, t: (step(b, h, t), 0))],
        out_shape=[jax.ShapeDtypeStruct((m, hv * RET_HEADS), BF16),
                   jax.ShapeDtypeStruct((kdim, d), BF16)],
        scratch_shapes=[pltpu.VMEM((hq, hv), F32)],
        compiler_params=_params("arbitrary", "arbitrary", "arbitrary"),
        name="ret_core",
    )(qk, qk, vg, vg, dmask, qdec, kdec, cdec, w_out)


def _retention_constants(head_qk, chunk):
    log_gamma = jnp.log1p(-jnp.exp2(-5.0 - jnp.arange(RET_HEADS, dtype=F32)))
    idx = jnp.arange(chunk, dtype=F32)
    rel = idx[:, None] - idx[None, :]
    dmask = jnp.where(rel[None] >= 0,
                      jnp.exp(jnp.maximum(rel, 0.0)[None] * log_gamma[:, None, None]), 0.0)
    qdec = jnp.exp((idx[None, :] + 1.0) * log_gamma[:, None])[:, :, None]
    kdec = jnp.exp((chunk - 1.0 - idx)[None, :] * log_gamma[:, None])[:, :, None]
    cdec = jnp.exp(chunk * log_gamma)[:, None, None]
    inv_freq = 1.0 / (ROPE_BASE ** jnp.linspace(0.0, 1.0, head_qk // 2, dtype=F32))
    inv_freq = jnp.repeat(inv_freq, 2)[None, :]
    return dmask, qdec, kdec, cdec, inv_freq


def kernel(x, positions, mix_pre_g, mix_post_g, gmlp_w_in, gmlp_ln_g, gmlp_ln_b, gmlp_w_s, gmlp_b_s,
           gmlp_w_out, ret_w_in, ret_w_out, ffn_pre_g, ffn_post_g, ffn_w_up, ffn_conv_w, ffn_conv_b,
           ffn_w_down):
    batch, seq, d = x.shape
    m = batch * seq
    qk_dim = d
    v_dim = (ret_w_in.shape[2] - 2 * qk_dim) // 2
    head_qk = qk_dim // RET_HEADS
    xf = x.reshape(m, d)

    z, mu, rstd, w_bf = _gmlp_in(xf, mix_pre_g[0], gmlp_w_in.astype(BF16), gmlp_w_out, bm=1024, bn=2048)
    xf, h = _gmlp_down(z, mu, rstd, gmlp_ln_g[0], gmlp_ln_b[0], gmlp_w_s[0], gmlp_b_s[0],
                       w_bf, xf, mix_post_g[0], ffn_pre_g[0], bm=256)
    ret_w_in_side = _cast_rows_side(ret_w_in, 0, n_used=64, keep_layer_axis=True)
    a, w_bf, ret_w_in_bf = _ffn_up(h, ffn_w_up, ffn_conv_w, ffn_conv_b, ffn_w_down, 0, seq=seq, bm=2048,
                                   tn=512, row_subtiles=2, more_sides=[ret_w_in_side])
    xf, h = _down_proj(a, w_bf, xf, ffn_post_g[0], mix_pre_g[1], bm=256)

    dmask, qdec, kdec, cdec, inv_freq = _retention_constants(head_qk, RET_CHUNK)
    cos, sin = _rope_tables(positions.reshape(m, 1), inv_freq, bm=1024)
    qk, vg = _ret_in(h, ret_w_in_bf, cos, sin, bm=1024, bn=2048,
                     qk_dim=qk_dim, v_dim=v_dim, head_qk=head_qk)
    o, w_bf = _ret_core(qk, vg, dmask, qdec, kdec, cdec, ret_w_out, batch=batch, seq=seq, ts=4096)
    xf, h = _down_proj(o, w_bf, xf, mix_post_g[1], ffn_pre_g[1], bm=256)
    a, w_bf = _ffn_up(h, ffn_w_up, ffn_conv_w, ffn_conv_b, ffn_w_down, 1, seq=seq, bm=2048, tn=512,
                      row_subtiles=2)
    xf, _ = _down_proj(a, w_bf, xf, ffn_post_g[1], bm=256)
    return xf.reshape(batch, seq, d)
```

```python
import functools

import jax
import jax.numpy as jnp
import numpy as np
from jax import lax
from jax.experimental import pallas as pl
from jax.experimental.pallas import tpu as pltpu

EPS = 1e-6
CHUNK = 128
RET_CHUNK = 256
GMLP_GROUPS = 8
RET_HEADS = 8
ROPE_BASE = 10000.0
CONV_WIDTH = 3

F32 = jnp.float32
BF16 = jnp.bfloat16

V7X_VMEM_BYTES = 64 * 1024 * 1024
VMEM_LIMIT_BYTES = V7X_VMEM_BYTES - 8 * 1024 * 1024
LANES = 128
SUBLANES = 8


def _params(*semantics):
    return pltpu.CompilerParams(dimension_semantics=semantics, vmem_limit_bytes=VMEM_LIMIT_BYTES)


def _rms(x):
    return x * lax.rsqrt(jnp.mean(x * x, axis=-1, keepdims=True) + EPS)


def _staged_pipeline(s, n, stages):
    depth = len(stages)

    def run(active, step_parity):
        for k in active:
            stages[k]((step_parity - k) % 2)

    for fill in range(depth - 1):
        pl.when(s == fill)(functools.partial(run, range(fill + 1), fill % 2))
    for parity in (0, 1):
        pl.when((s >= depth - 1) & (s < n) & (s % 2 == parity))(
            functools.partial(run, range(depth), parity))
    for drain in range(1, depth):
        step = n - 1 + drain
        pl.when(s == step)(functools.partial(run, range(drain, depth), step % 2))


def _up_kernel(*refs, n_w, n_extra, n_out, side_jobs, weight_stationary, cast_weights, row_subtiles,
               prenorm, epilogue):
    h_ref = refs[0]
    if prenorm:
        gain_ref, refs = refs[1], refs[:1] + refs[2:]
    w_refs = refs[1:1 + n_w]
    pos = 1 + n_w
    extra_refs = refs[pos:pos + n_extra]
    pos += n_extra
    side_in_refs = []
    for n_in, _, _ in side_jobs:
        side_in_refs.append(refs[pos:pos + n_in])
        pos += n_in
    out_refs = refs[pos:pos + n_out]
    pos += n_out
    side_out_refs = []
    for _, n_o, _ in side_jobs:
        side_out_refs.append(refs[pos:pos + n_o])
        pos += n_o
    if cast_weights:
        wbf_refs = refs[pos:pos + n_w]
        pos += n_w
    else:
        wbf_refs = w_refs
    if prenorm:
        hn_ref = refs[pos]
        pos += 1
    epi_scratch = refs[pos:]

    if weight_stationary:
        j, i = pl.program_id(0), pl.program_id(1)
    else:
        i, j = pl.program_id(0), pl.program_id(1)

    if prenorm:
        @pl.when(j == 0)
        def _():
            hn_ref[...] = (_rms(h_ref[...]) * gain_ref[...]).astype(BF16)

        h_ref = hn_ref

    if cast_weights:
        @pl.when(i == 0)
        def _():
            for w_ref, wbf_ref in zip(w_refs, wbf_refs):
                wbf_ref[...] = w_ref[...].astype(BF16)

    if epi_scratch:
        @pl.when((i == 0) & (j == 0))
        def _():
            for ref in epi_scratch:
                ref[...] = jnp.zeros_like(ref)

    if row_subtiles == 1:
        h = h_ref[...]
        ys = [jnp.dot(h, wbf_ref[...], preferred_element_type=F32) for wbf_ref in wbf_refs]
        epilogue(ys, extra_refs, out_refs, epi_scratch, i, j)
    else:
        sub = h_ref.shape[0] // row_subtiles

        def row_tile(r, carry):
            rows = pl.ds(pl.multiple_of(r * sub, sub), sub)
            h = h_ref[rows, :]
            ys = [jnp.dot(h, wbf_ref[...], preferred_element_type=F32) for wbf_ref in wbf_refs]
            epilogue(ys, extra_refs, [o.at[rows, :] for o in out_refs], epi_scratch,
                     i * row_subtiles + r, j)
            return carry

        lax.fori_loop(0, row_subtiles, row_tile, 0)
    for (_, _, fn), ins, outs in zip(side_jobs, side_in_refs, side_out_refs):
        fn(ins, outs)


def _up_proj(h, weights, *, n_col_tiles, bm, bn, weight_stationary, epilogue, extras=(), outs,
             epi_scratch=(), sides=(), row_subtiles=1, prenorm_gain=None, name):
    m, d = h.shape
    n_i, n_j = m // bm, n_col_tiles
    cast_weights = weights[0][0].dtype != BF16
    prenorm = prenorm_gain is not None
    assert weight_stationary or not cast_weights
    assert not (prenorm and weight_stationary)
    if weight_stationary:
        grid = (n_j, n_i)

        def ij(fn):
            return lambda j, i: fn(i, j)

        def step(fn):
            return lambda j, i: fn(j * n_i + i)
    else:
        grid = (n_i, n_j)

        def ij(fn):
            return fn

        def step(fn):
            return lambda i, j: fn(i * n_j + j)

    built = [side(n_i * n_j) for side in sides]
    side_ins = [entry for ins, _, _ in built for entry in ins]
    side_outs = [entry for _, outs_, _ in built for entry in outs_]
    in_specs = [pl.BlockSpec((bm, d), ij(lambda i, j: (i, 0)))]
    if prenorm:
        in_specs.append(pl.BlockSpec((1, d), ij(lambda i, j: (0, 0))))
    for _, layer, c0 in weights:
        in_specs.append(pl.BlockSpec((None, d, bn), ij(lambda i, j, layer=layer, c0=c0: (layer, 0, j + c0))))
    in_specs += [pl.BlockSpec(shape, ij(fn)) for _, shape, fn in extras]
    in_specs += [pl.BlockSpec(shape, step(fn)) for _, shape, fn in side_ins]
    out_specs = [pl.BlockSpec(shape, ij(fn)) for _, shape, fn in outs]
    out_specs += [pl.BlockSpec(shape, step(fn)) for _, shape, fn in side_outs]
    scratch = []
    if cast_weights:
        scratch += [pltpu.VMEM((d, bn), BF16) for _ in weights]
    if prenorm:
        scratch.append(pltpu.VMEM((bm, d), BF16))
    scratch += list(epi_scratch)
    kern = functools.partial(_up_kernel, n_w=len(weights), n_extra=len(extras), n_out=len(outs),
                             side_jobs=[(len(ins), len(outs_), fn) for ins, outs_, fn in built],
                             weight_stationary=weight_stationary, cast_weights=cast_weights,
                             row_subtiles=row_subtiles, prenorm=prenorm, epilogue=epilogue)
    res = pl.pallas_call(
        kern,
        grid=grid,
        in_specs=in_specs,
        out_specs=out_specs,
        out_shape=[sds for sds, _, _ in outs] + [sds for sds, _, _ in side_outs],
        scratch_shapes=scratch,
        compiler_params=_params("arbitrary", "arbitrary"),
        name=name,
    )(h, *([prenorm_gain.reshape(1, d)] if prenorm else []), *[w for w, _, _ in weights],
      *[a for a, _, _ in extras], *[a for a, _, _ in side_ins])
    return res[:len(outs)], res[len(outs):]


def _cast_rows_side(w, layer, *, n_used=None, keep_layer_axis=False):
    kdim, d = w.shape[1:]

    def build(n_steps):
        used = n_steps if n_used is None else n_used
        rows = kdim // used
        assert used <= n_steps and rows * used == kdim and rows % (2 * SUBLANES) == 0, (kdim, used)

        def fn(in_refs, out_refs):
            out_refs[0][...] = in_refs[0][...].astype(BF16)

        def block(t):
            return jnp.minimum(t, used - 1)

        if keep_layer_axis:
            out = (jax.ShapeDtypeStruct((1, kdim, d), BF16), (None, rows, d), lambda t: (0, block(t), 0))
        else:
            out = (jax.ShapeDtypeStruct((kdim, d), BF16), (rows, d), lambda t: (block(t), 0))
        return [(w, (None, rows, d), lambda t: (layer, block(t), 0))], [out], fn

    return build


def _lane_partial_sum(y):
    acc = y[:, 0:LANES]
    for c in range(1, y.shape[1] // LANES):
        acc = acc + y[:, c * LANES:(c + 1) * LANES]
    return acc


def _gelu_exact(y):
    return 0.5 * y * (1.0 + lax.erf(y * np.float32(np.sqrt(0.5))))


def _gmlp_in_epilogue(ys, extras, outs, scratch, ip, jp, *, n_u_tiles, n_tiles, v_width):
    z_ref, mu_ref, rstd_ref = outs
    s_ref, ss_ref = scratch
    z = _gelu_exact(ys[0])
    z_ref[...] = z.astype(z_ref.dtype)
    carry = jp > n_u_tiles
    in_v = jp >= n_u_tiles
    s_ref[...] = jnp.where(carry, s_ref[...], 0.0) + jnp.where(in_v, _lane_partial_sum(z), 0.0)
    ss_ref[...] = jnp.where(carry, ss_ref[...], 0.0) + jnp.where(in_v, _lane_partial_sum(z * z), 0.0)

    @pl.when(jp == n_tiles - 1)
    def _():
        mu = jnp.sum(s_ref[...], axis=-1, keepdims=True) / v_width
        var = jnp.sum(ss_ref[...], axis=-1, keepdims=True) / v_width - mu * mu
        mu_ref[...] = mu
        rstd_ref[...] = lax.rsqrt(var + EPS)


def _gmlp_in(x, pre_g, w, w_out, *, bm, bn):
    h = x
    m = h.shape[0]
    n = w.shape[2]
    n_tiles = n // bn
    epilogue = functools.partial(_gmlp_in_epilogue, n_u_tiles=n_tiles // 2, n_tiles=n_tiles,
                                 v_width=n // 2)
    (z, mu, rstd), (w_out_bf,) = _up_proj(
        h, [(w, 0, 0)], n_col_tiles=n_tiles, bm=bm, bn=bn, weight_stationary=False, epilogue=epilogue,
        outs=[(jax.ShapeDtypeStruct((m, n), BF16), (bm, bn), lambda i, j: (i, j)),
              (jax.ShapeDtypeStruct((m, 1), F32), (bm, 1), lambda i, j: (i, 0)),
              (jax.ShapeDtypeStruct((m, 1), F32), (bm, 1), lambda i, j: (i, 0))],
        epi_scratch=[pltpu.VMEM((bm, LANES), F32), pltpu.VMEM((bm, LANES), F32)],
        sides=[_cast_rows_side(w_out, 0)], prenorm_gain=pre_g, name="gmlp_in")
    return z, mu, rstd, w_out_bf


def _causal_conv(z, prev, cw, cb):
    row = lax.broadcasted_iota(jnp.int32, z.shape, 0)
    z1 = jnp.where(row == 0, prev[7:8, :], pltpu.roll(z, 1, 0))
    z2 = jnp.where(row == 0, prev[6:7, :], jnp.where(row == 1, prev[7:8, :], pltpu.roll(z, 2, 0)))
    return cb + cw[0:1, :] * z2 + cw[1:2, :] * z1 + cw[2:3, :] * z


def _ffn_up_epilogue(ys, extras, outs, scratch, ip, jp, *, tiles_per_seq):
    cwg_ref, cwu_ref, cbg_ref, cbu_ref = extras
    seq_start = ip % tiles_per_seq == 0

    def branch(z, cw_ref, cb_ref, carry_ref):
        prev = jnp.where(seq_start, 0.0, carry_ref[...])
        out = _causal_conv(z, prev, cw_ref[...], cb_ref[...])
        carry_ref[...] = z[z.shape[0] - SUBLANES:, :]
        return out

    gate = branch(ys[0], cwg_ref, cbg_ref, scratch[0])
    up = branch(ys[1], cwu_ref, cbu_ref, scratch[1])
    outs[0][...] = (jax.nn.silu(gate) * up).astype(outs[0].dtype)


def _ffn_up(h, w, conv_w, conv_b, w_down, layer, *, seq, bm, tn, row_subtiles, more_sides=()):
    m = h.shape[0]
    f = w.shape[2] // 2
    nj = f // tn
    cb = conv_b.reshape(conv_b.shape[0], 1, 2 * f)
    (a,), side_outs = _up_proj(
        h, [(w, layer, 0), (w, layer, nj)], n_col_tiles=nj, bm=bm, bn=tn, weight_stationary=True,
        epilogue=functools.partial(_ffn_up_epilogue, tiles_per_seq=seq * row_subtiles // bm),
        extras=[(conv_w, (None, CONV_WIDTH, tn), lambda i, j: (layer, 0, j)),
                (conv_w, (None, CONV_WIDTH, tn), lambda i, j: (layer, 0, j + nj)),
                (cb, (None, 1, tn), lambda i, j: (layer, 0, j)),
                (cb, (None, 1, tn), lambda i, j: (layer, 0, j + nj))],
        outs=[(jax.ShapeDtypeStruct((m, f), BF16), (bm, tn), lambda i, j: (i, j))],
        epi_scratch=[pltpu.VMEM((SUBLANES, tn), F32), pltpu.VMEM((SUBLANES, tn), F32)],
        sides=[_cast_rows_side(w_down, layer), *more_sides], row_subtiles=row_subtiles, name="ffn_up")
    return (a, *side_outs)


def _rope_epilogue(ys, extras, outs, scratch, ip, jp, *, n_q_tiles, head_qk):
    cos = extras[0][...]
    sin = extras[1][...]
    y, o_ref = ys[0], outs[0]
    even = (lax.broadcasted_iota(jnp.int32, cos.shape, 1) & 1) == 0
    scale = jnp.where(jp >= n_q_tiles, head_qk ** -0.5, 1.0).astype(F32)
    for hh in range(y.shape[1] // head_qk):
        t = y[:, hh * head_qk:(hh + 1) * head_qk]
        nxt = pltpu.roll(t, head_qk - 1, 1)
        prv = pltpu.roll(t, 1, 1)
        r = (t * cos + jnp.where(even, -nxt, prv) * sin) * scale
        o_ref[hh] = r.astype(o_ref.dtype)


def _value_gate_epilogue(ys, extras, outs, scratch, ip, jp, *, n_v_tiles):
    o_ref = outs[0]
    head_v = o_ref.shape[2]
    y = ys[0]
    y = jnp.where(jp >= n_v_tiles, jax.nn.silu(y), y).astype(o_ref.dtype)
    for hh in range(o_ref.shape[0]):
        o_ref[hh] = y[:, hh * head_v:(hh + 1) * head_v]


def _ret_in(h, w, cos, sin, *, bm, bn, qk_dim, v_dim, head_qk):
    m = h.shape[0]
    nq = qk_dim // bn
    nv = v_dim // bn
    head_v = v_dim // RET_HEADS
    tab = (bm, head_qk)
    qk_heads = bn // head_qk
    vg_heads = bn // head_v
    (qk,), _ = _up_proj(
        h, [(w, 0, 0)], n_col_tiles=2 * nq, bm=bm, bn=bn, weight_stationary=True,
        epilogue=functools.partial(_rope_epilogue, n_q_tiles=nq, head_qk=head_qk),
        extras=[(cos, tab, lambda i, j: (i, 0)), (sin, tab, lambda i, j: (i, 0))],
        outs=[(jax.ShapeDtypeStruct((2 * RET_HEADS, m, head_qk), BF16), (qk_heads, bm, head_qk),
               lambda i, j: (j, i, 0))],
        name="ret_in_qk")
    (vg,), _ = _up_proj(
        h, [(w, 0, 2 * nq)], n_col_tiles=2 * nv, bm=bm, bn=bn, weight_stationary=True,
        epilogue=functools.partial(_value_gate_epilogue, n_v_tiles=nv),
        outs=[(jax.ShapeDtypeStruct((2 * RET_HEADS, m, head_v), BF16), (vg_heads, bm, head_v),
               lambda i, j: (j, i, 0))],
        name="ret_in_vg")
    return qk, vg


def _residual_norm_epilogue(acc, x_ref, pg_ref, ng_ref, xo_ref, ho_ref):
    xn = x_ref[...] + _rms(acc) * pg_ref[...]
    xo_ref[...] = xn
    if ho_ref is not None:
        ho_ref[...] = (_rms(xn) * ng_ref[...]).astype(ho_ref.dtype)


def _down_kernel(a_ref, w_ref, x_ref, pg_ref, *rest, with_next):
    if with_next:
        ng_ref, xo_ref, ho_ref, raw0_ref, raw1_ref = rest
    else:
        (xo_ref, raw0_ref, raw1_ref), ng_ref, ho_ref = rest, None, None
    raws = (raw0_ref, raw1_ref)

    def matmul(par):
        raws[par][...] = jnp.dot(a_ref[...], w_ref[...], preferred_element_type=F32)

    def finish(par):
        _residual_norm_epilogue(raws[par][...], x_ref, pg_ref, ng_ref, xo_ref, ho_ref)

    _staged_pipeline(pl.program_id(0), pl.num_programs(0) - 1, [matmul, finish])


def _down_proj(a, w, x, post_g, next_g=None, *, bm):
    m, kdim = a.shape
    d = w.shape[1]
    n = m // bm
    with_next = next_g is not None
    prev_row = pl.BlockSpec((bm, d), lambda s: (jnp.maximum(s - 1, 0), 0))
    gain = pl.BlockSpec((1, d), lambda s: (0, 0))
    gains = [post_g.reshape(1, d)] + ([next_g.reshape(1, d)] if with_next else [])
    out_shape = [jax.ShapeDtypeStruct((m, d), F32)]
    if with_next:
        out_shape.append(jax.ShapeDtypeStruct((m, d), BF16))
    res = pl.pallas_call(
        functools.partial(_down_kernel, with_next=with_next),
        grid=(n + 1,),
        in_specs=[
            pl.BlockSpec((bm, kdim), lambda s: (jnp.minimum(s, n - 1), 0)),
            pl.BlockSpec((kdim, d), lambda s: (0, 0)),
            prev_row,
        ] + [gain] * len(gains),
        out_specs=[prev_row] * len(out_shape),
        out_shape=out_shape,
        scratch_shapes=[pltpu.VMEM((bm, d), F32), pltpu.VMEM((bm, d), F32)],
        compiler_params=_params("arbitrary"),
        name="down_proj",
    )(a, w, x, *gains)
    return res if with_next else (res[0], None)


def _gmlp_down_kernel(u_ref, v_ref, mu_ref, rstd_ref, lg_ref, lb_ref, ws_ref, bs_ref, w_ref,
                      x_ref, pg_ref, ng_ref, xo_ref, ho_ref, raw0_ref, raw1_ref):
    raws = (raw0_ref, raw1_ref)
    bm, half = u_ref.shape
    gd = half // GMLP_GROUPS

    def gate_and_project(par):
        tril = (lax.broadcasted_iota(jnp.int32, (CHUNK, CHUNK), 0)
                >= lax.broadcasted_iota(jnp.int32, (CHUNK, CHUNK), 1))
        mu = mu_ref[...]
        rstd = rstd_ref[...]
        acc = None
        for g in range(GMLP_GROUPS):
            cols = slice(g * gd, (g + 1) * gd)
            ws = jnp.where(tril, ws_ref[g], 0.0).astype(BF16)
            bias = bs_ref[g]
            vn = (v_ref[:, cols].astype(F32) - mu) * rstd * lg_ref[:, cols] + lb_ref[:, cols]
            vn = vn.astype(BF16)
            gated = []
            for c in range(bm // CHUNK):
                rows = slice(c * CHUNK, (c + 1) * CHUNK)
                mixed = jnp.dot(ws, vn[rows], preferred_element_type=F32) + bias
                gated.append((u_ref[rows, cols].astype(F32) * mixed).astype(BF16))
            part = jnp.dot(jnp.concatenate(gated, axis=0), w_ref[cols, :], preferred_element_type=F32)
            acc = part if acc is None else acc + part
        raws[par][...] = acc

    def finish(par):
        _residual_norm_epilogue(raws[par][...], x_ref, pg_ref, ng_ref, xo_ref, ho_ref)

    _staged_pipeline(pl.program_id(0), pl.num_programs(0) - 1, [gate_and_project, finish])


def _gmlp_down(z, mu, rstd, ln_g, ln_b, w_s, b_s, w, x, post_g, next_g, *, bm):
    m = z.shape[0]
    half, d = w.shape
    n = m // bm

    def first(s):
        return jnp.minimum(s, n - 1)

    def last(s):
        return jnp.maximum(s - 1, 0)

    row = pl.BlockSpec((bm, d), lambda s: (last(s), 0))
    gain = pl.BlockSpec((1, d), lambda s: (0, 0))
    stat = pl.BlockSpec((bm, 1), lambda s: (first(s), 0))
    lnp = pl.BlockSpec((1, half), lambda s: (0, 0))
    return pl.pallas_call(
        _gmlp_down_kernel,
        grid=(n + 1,),
        in_specs=[
            pl.BlockSpec((bm, half), lambda s: (first(s), 0)),
            pl.BlockSpec((bm, half), lambda s: (first(s), 1)),
            stat, stat, lnp, lnp,
            pl.BlockSpec((GMLP_GROUPS, CHUNK, CHUNK), lambda s: (0, 0, 0)),
            pl.BlockSpec((GMLP_GROUPS, CHUNK, 1), lambda s: (0, 0, 0)),
            pl.BlockSpec((half, d), lambda s: (0, 0)),
            row, gain, gain,
        ],
        out_specs=[row, row],
        out_shape=[jax.ShapeDtypeStruct((m, d), F32), jax.ShapeDtypeStruct((m, d), BF16)],
        scratch_shapes=[pltpu.VMEM((bm, d), F32), pltpu.VMEM((bm, d), F32)],
        compiler_params=_params("arbitrary"),
        name="gmlp_down",
    )(z, z, mu, rstd, ln_g.reshape(1, half), ln_b.reshape(1, half), w_s,
      b_s.reshape(GMLP_GROUPS, CHUNK, 1), w, x, post_g.reshape(1, d), next_g.reshape(1, d))


def _rope_table_kernel(pos_ref, inv_ref, w_ref, cos_ref, sin_ref, wo_ref):
    ang = pos_ref[...].astype(F32) * inv_ref[...]
    cos_ref[...] = jnp.cos(ang)
    sin_ref[...] = jnp.sin(ang)
    wo_ref[...] = w_ref[...].astype(BF16)


def _rope_tables(pos, inv_freq, w, *, bm):
    m = pos.shape[0]
    dk = inv_freq.shape[1]
    n_steps = m // bm
    layers, kdim, n = w.shape
    wrows = kdim // n_steps
    assert wrows * n_steps == kdim and wrows % (2 * SUBLANES) == 0, (kdim, n_steps)
    tab = pl.BlockSpec((bm, dk), lambda i: (i, 0))
    wblk = pl.BlockSpec((layers, wrows, n), lambda i: (0, i, 0))
    return pl.pallas_call(
        _rope_table_kernel,
        grid=(n_steps,),
        in_specs=[pl.BlockSpec((bm, 1), lambda i: (i, 0)), pl.BlockSpec((1, dk), lambda i: (0, 0)), wblk],
        out_specs=[tab, tab, wblk],
        out_shape=[jax.ShapeDtypeStruct((m, dk), F32)] * 2 + [jax.ShapeDtypeStruct(w.shape, BF16)],
        compiler_params=_params("arbitrary"),
        name="rope_tables",
    )(pos, inv_freq, w)


def _ret_core_kernel(q_ref, k_ref, v_ref, g_ref, dm_ref, qd_ref, kd_ref, cd_ref, wsrc_ref,
                     o_ref, wdst_ref, state_ref):
    t = pl.program_id(2)

    @pl.when(t == 0)
    def _():
        state_ref[...] = jnp.zeros_like(state_ref)

    wdst_ref[...] = wsrc_ref[...].astype(BF16)
    dmask = dm_ref[0]
    qdec = qd_ref[0]
    kdec = kd_ref[0]
    cdec = cd_ref[0]
    chunk = dmask.shape[0]
    state = state_ref[...]
    for c in range(q_ref.shape[0] // chunk):
        rows = slice(c * chunk, (c + 1) * chunk)
        qc = q_ref[rows, :]
        kc = k_ref[rows, :]
        vc = v_ref[rows, :]
        scores = lax.dot_general(qc, kc, (((1,), (1,)), ((), ())), preferred_element_type=F32) * dmask
        intra = jnp.dot(scores.astype(BF16), vc, preferred_element_type=F32)
        qd = (qc.astype(F32) * qdec).astype(BF16)
        cross = jnp.dot(qd, state.astype(BF16), preferred_element_type=F32)
        kd = (kc.astype(F32) * kdec).astype(BF16)
        upd = lax.dot_general(kd, vc, (((0,), (0,)), ((), ())), preferred_element_type=F32)
        state = state * cdec + upd
        o = _rms(intra + cross)
        o_ref[rows, :] = (g_ref[rows, :].astype(F32) * o).astype(o_ref.dtype)
    state_ref[...] = state


def _ret_core(qk, vg, dmask, qdec, kdec, cdec, w_out, *, batch, seq, ts):
    m, hq = qk.shape[1:]
    hv = vg.shape[2]
    chunk = dmask.shape[1]
    nts = seq // ts
    kdim, d = w_out.shape[1:]
    n_steps = batch * RET_HEADS * nts
    wrows = kdim // n_steps
    assert wrows * n_steps == kdim and wrows % (2 * SUBLANES) == 0, (kdim, n_steps)

    def step(b, h, t):
        return (b * RET_HEADS + h) * nts + t

    return pl.pallas_call(
        _ret_core_kernel,
        grid=(batch, RET_HEADS, nts),
        in_specs=[
            pl.BlockSpec((None, ts, hq), lambda b, h, t: (h, b * nts + t, 0)),
            pl.BlockSpec((None, ts, hq), lambda b, h, t: (RET_HEADS + h, b * nts + t, 0)),
            pl.BlockSpec((None, ts, hv), lambda b, h, t: (h, b * nts + t, 0)),
            pl.BlockSpec((None, ts, hv), lambda b, h, t: (RET_HEADS + h, b * nts + t, 0)),
            pl.BlockSpec((1, chunk, chunk), lambda b, h, t: (h, 0, 0)),
            pl.BlockSpec((1, chunk, 1), lambda b, h, t: (h, 0, 0)),
            pl.BlockSpec((1, chunk, 1), lambda b, h, t: (h, 0, 0)),
            pl.BlockSpec((1, 1, 1), lambda b, h, t: (h, 0, 0)),
            pl.BlockSpec((None, wrows, d), lambda b, h, t: (0, step(b, h, t), 0)),
        ],
        out_specs=[pl.BlockSpec((ts, hv), lambda b, h, t: (b * nts + t, h)),
                   pl.BlockSpec((wrows, d), lambda b, h, t: (step(b, h, t), 0))],
        out_shape=[jax.ShapeDtypeStruct((m, hv * RET_HEADS), BF16),
                   jax.ShapeDtypeStruct((kdim, d), BF16)],
        scratch_shapes=[pltpu.VMEM((hq, hv), F32)],
        compiler_params=_params("arbitrary", "arbitrary", "arbitrary"),
        name="ret_core",
    )(qk, qk, vg, vg, dmask, qdec, kdec, cdec, w_out)


def _retention_constants(head_qk, chunk):
    log_gamma = jnp.log1p(-jnp.exp2(-5.0 - jnp.arange(RET_HEADS, dtype=F32)))
    idx = jnp.arange(chunk, dtype=F32)
    rel = idx[:, None] - idx[None, :]
    dmask = jnp.where(rel[None] >= 0,
                      jnp.exp(jnp.maximum(rel, 0.0)[None] * log_gamma[:, None, None]), 0.0)
    qdec = jnp.exp((idx[None, :] + 1.0) * log_gamma[:, None])[:, :, None]
    kdec = jnp.exp((chunk - 1.0 - idx)[None, :] * log_gamma[:, None])[:, :, None]
    cdec = jnp.exp(chunk * log_gamma)[:, None, None]
    inv_freq = 1.0 / (ROPE_BASE ** jnp.linspace(0.0, 1.0, head_qk // 2, dtype=F32))
    inv_freq = jnp.repeat(inv_freq, 2)[None, :]
    return dmask, qdec, kdec, cdec, inv_freq


def kernel(x, positions, mix_pre_g, mix_post_g, gmlp_w_in, gmlp_ln_g, gmlp_ln_b, gmlp_w_s, gmlp_b_s,
           gmlp_w_out, ret_w_in, ret_w_out, ffn_pre_g, ffn_post_g, ffn_w_up, ffn_conv_w, ffn_conv_b,
           ffn_w_down):
    batch, seq, d = x.shape
    m = batch * seq
    qk_dim = d
    v_dim = (ret_w_in.shape[2] - 2 * qk_dim) // 2
    head_qk = qk_dim // RET_HEADS
    xf = x.reshape(m, d)

    dmask, qdec, kdec, cdec, inv_freq = _retention_constants(head_qk, RET_CHUNK)
    cos, sin, gmlp_w_in_bf = _rope_tables(positions.reshape(m, 1), inv_freq, gmlp_w_in, bm=1024)

    z, mu, rstd, w_bf = _gmlp_in(xf, mix_pre_g[0], gmlp_w_in_bf, gmlp_w_out, bm=1024, bn=2048)
    xf, h = _gmlp_down(z, mu, rstd, gmlp_ln_g[0], gmlp_ln_b[0], gmlp_w_s[0], gmlp_b_s[0],
                       w_bf, xf, mix_post_g[0], ffn_pre_g[0], bm=256)
    ret_w_in_side = _cast_rows_side(ret_w_in, 0, n_used=64, keep_layer_axis=True)
    a, w_bf, ret_w_in_bf = _ffn_up(h, ffn_w_up, ffn_conv_w, ffn_conv_b, ffn_w_down, 0, seq=seq, bm=2048,
                                   tn=512, row_subtiles=2, more_sides=[ret_w_in_side])
    xf, h = _down_proj(a, w_bf, xf, ffn_post_g[0], mix_pre_g[1], bm=256)

    qk, vg = _ret_in(h, ret_w_in_bf, cos, sin, bm=1024, bn=2048,
                     qk_dim=qk_dim, v_dim=v_dim, head_qk=head_qk)
    o, w_bf = _ret_core(qk, vg, dmask, qdec, kdec, cdec, ret_w_out, batch=batch, seq=seq, ts=4096)
    xf, h = _down_proj(o, w_bf, xf, mix_post_g[1], ffn_pre_g[1], bm=256)
    a, w_bf = _ffn_up(h, ffn_w_up, ffn_conv_w, ffn_conv_b, ffn_w_down, 1, seq=seq, bm=2048, tn=512,
                      row_subtiles=2)
    xf, _ = _down_proj(a, w_bf, xf, ffn_post_g[1], bm=256)
    return xf.reshape(batch, seq, d)
```

```python
import functools

import jax
import jax.numpy as jnp
import numpy as np
from jax import lax
from jax.experimental import pallas as pl
from jax.experimental.pallas import tpu as pltpu

EPS = 1e-6
CHUNK = 128
RET_CHUNK = 256
GMLP_GROUPS = 8
RET_HEADS = 8
ROPE_BASE = 10000.0
CONV_WIDTH = 3

F32 = jnp.float32
BF16 = jnp.bfloat16

V7X_VMEM_BYTES = 64 * 1024 * 1024
VMEM_LIMIT_BYTES = V7X_VMEM_BYTES - 8 * 1024 * 1024
LANES = 128
SUBLANES = 8


def _params(*semantics):
    return pltpu.CompilerParams(dimension_semantics=semantics, vmem_limit_bytes=VMEM_LIMIT_BYTES)


def _rms(x):
    return x * lax.rsqrt(jnp.mean(x * x, axis=-1, keepdims=True) + EPS)


def _staged_pipeline(s, n, stages):
    depth = len(stages)

    def run(active, step_parity):
        for k in active:
            stages[k]((step_parity - k) % 2)

    for fill in range(depth - 1):
        pl.when(s == fill)(functools.partial(run, range(fill + 1), fill % 2))
    for parity in (0, 1):
        pl.when((s >= depth - 1) & (s < n) & (s % 2 == parity))(
            functools.partial(run, range(depth), parity))
    for drain in range(1, depth):
        step = n - 1 + drain
        pl.when(s == step)(functools.partial(run, range(drain, depth), step % 2))


def _up_kernel(*refs, n_w, n_extra, n_out, side_jobs, weight_stationary, cast_weights, row_subtiles,
               prenorm, epilogue):
    h_ref = refs[0]
    if prenorm:
        gain_ref, refs = refs[1], refs[:1] + refs[2:]
    w_refs = refs[1:1 + n_w]
    pos = 1 + n_w
    extra_refs = refs[pos:pos + n_extra]
    pos += n_extra
    side_in_refs = []
    for n_in, _, _ in side_jobs:
        side_in_refs.append(refs[pos:pos + n_in])
        pos += n_in
    out_refs = refs[pos:pos + n_out]
    pos += n_out
    side_out_refs = []
    for _, n_o, _ in side_jobs:
        side_out_refs.append(refs[pos:pos + n_o])
        pos += n_o
    if cast_weights:
        wbf_refs = refs[pos:pos + n_w]
        pos += n_w
    else:
        wbf_refs = w_refs
    if prenorm:
        hn_ref = refs[pos]
        pos += 1
    epi_scratch = refs[pos:]

    if weight_stationary:
        j, i = pl.program_id(0), pl.program_id(1)
    else:
        i, j = pl.program_id(0), pl.program_id(1)

    if prenorm:
        @pl.when(j == 0)
        def _():
            hn_ref[...] = (_rms(h_ref[...]) * gain_ref[...]).astype(BF16)

        h_ref = hn_ref

    if cast_weights:
        @pl.when(i == 0)
        def _():
            for w_ref, wbf_ref in zip(w_refs, wbf_refs):
                wbf_ref[...] = w_ref[...].astype(BF16)

    if epi_scratch:
        @pl.when((i == 0) & (j == 0))
        def _():
            for ref in epi_scratch:
                ref[...] = jnp.zeros_like(ref)

    if row_subtiles == 1:
        h = h_ref[...]
        ys = [jnp.dot(h, wbf_ref[...], preferred_element_type=F32) for wbf_ref in wbf_refs]
        epilogue(ys, extra_refs, out_refs, epi_scratch, i, j)
    else:
        sub = h_ref.shape[0] // row_subtiles

        def row_tile(r, carry):
            rows = pl.ds(pl.multiple_of(r * sub, sub), sub)
            h = h_ref[rows, :]
            ys = [jnp.dot(h, wbf_ref[...], preferred_element_type=F32) for wbf_ref in wbf_refs]
            epilogue(ys, extra_refs, [o.at[rows, :] for o in out_refs], epi_scratch,
                     i * row_subtiles + r, j)
            return carry

        lax.fori_loop(0, row_subtiles, row_tile, 0)
    for (_, _, fn), ins, outs in zip(side_jobs, side_in_refs, side_out_refs):
        fn(ins, outs)


def _up_proj(h, weights, *, n_col_tiles, bm, bn, weight_stationary, epilogue, extras=(), outs,
             epi_scratch=(), sides=(), row_subtiles=1, prenorm_gain=None, name):
    m, d = h.shape
    n_i, n_j = m // bm, n_col_tiles
    cast_weights = weights[0][0].dtype != BF16
    prenorm = prenorm_gain is not None
    assert weight_stationary or not cast_weights
    assert not (prenorm and weight_stationary)
    if weight_stationary:
        grid = (n_j, n_i)

        def ij(fn):
            return lambda j, i: fn(i, j)

        def step(fn):
            return lambda j, i: fn(j * n_i + i)
    else:
        grid = (n_i, n_j)

        def ij(fn):
            return fn

        def step(fn):
            return lambda i, j: fn(i * n_j + j)

    built = [side(n_i * n_j) for side in sides]
    side_ins = [entry for ins, _, _ in built for entry in ins]
    side_outs = [entry for _, outs_, _ in built for entry in outs_]
    in_specs = [pl.BlockSpec((bm, d), ij(lambda i, j: (i, 0)))]
    if prenorm:
        in_specs.append(pl.BlockSpec((1, d), ij(lambda i, j: (0, 0))))
    for _, layer, c0 in weights:
        in_specs.append(pl.BlockSpec((None, d, bn), ij(lambda i, j, layer=layer, c0=c0: (layer, 0, j + c0))))
    in_specs += [pl.BlockSpec(shape, ij(fn)) for _, shape, fn in extras]
    in_specs += [pl.BlockSpec(shape, step(fn)) for _, shape, fn in side_ins]
    out_specs = [pl.BlockSpec(shape, ij(fn)) for _, shape, fn in outs]
    out_specs += [pl.BlockSpec(shape, step(fn)) for _, shape, fn in side_outs]
    scratch = []
    if cast_weights:
        scratch += [pltpu.VMEM((d, bn), BF16) for _ in weights]
    if prenorm:
        scratch.append(pltpu.VMEM((bm, d), BF16))
    scratch += list(epi_scratch)
    kern = functools.partial(_up_kernel, n_w=len(weights), n_extra=len(extras), n_out=len(outs),
                             side_jobs=[(len(ins), len(outs_), fn) for ins, outs_, fn in built],
                             weight_stationary=weight_stationary, cast_weights=cast_weights,
                             row_subtiles=row_subtiles, prenorm=prenorm, epilogue=epilogue)
    res = pl.pallas_call(
        kern,
        grid=grid,
        in_specs=in_specs,
        out_specs=out_specs,
        out_shape=[sds for sds, _, _ in outs] + [sds for sds, _, _ in side_outs],
        scratch_shapes=scratch,
        compiler_params=_params("arbitrary", "arbitrary"),
        name=name,
    )(h, *([prenorm_gain.reshape(1, d)] if prenorm else []), *[w for w, _, _ in weights],
      *[a for a, _, _ in extras], *[a for a, _, _ in side_ins])
    return res[:len(outs)], res[len(outs):]


def _cast_rows_side(w, layer, *, n_used=None, keep_layer_axis=False):
    kdim, d = w.shape[1:]

    def build(n_steps):
        used = n_steps if n_used is None else n_used
        rows = kdim // used
        assert used <= n_steps and rows * used == kdim and rows % (2 * SUBLANES) == 0, (kdim, used)

        def fn(in_refs, out_refs):
            out_refs[0][...] = in_refs[0][...].astype(BF16)

        def block(t):
            return jnp.minimum(t, used - 1)

        if keep_layer_axis:
            out = (jax.ShapeDtypeStruct((1, kdim, d), BF16), (None, rows, d), lambda t: (0, block(t), 0))
        else:
            out = (jax.ShapeDtypeStruct((kdim, d), BF16), (rows, d), lambda t: (block(t), 0))
        return [(w, (None, rows, d), lambda t: (layer, block(t), 0))], [out], fn

    return build


def _lane_partial_sum(y):
    acc = y[:, 0:LANES]
    for c in range(1, y.shape[1] // LANES):
        acc = acc + y[:, c * LANES:(c + 1) * LANES]
    return acc


def _gelu_exact(y):
    return 0.5 * y * (1.0 + lax.erf(y * np.float32(np.sqrt(0.5))))


def _gmlp_in_epilogue(ys, extras, outs, scratch, ip, jp, *, n_u_tiles, n_tiles, v_width):
    z_ref, mu_ref, rstd_ref = outs
    s_ref, ss_ref = scratch
    z = _gelu_exact(ys[0])
    z_ref[...] = z.astype(z_ref.dtype)
    carry = jp > n_u_tiles
    in_v = jp >= n_u_tiles
    s_ref[...] = jnp.where(carry, s_ref[...], 0.0) + jnp.where(in_v, _lane_partial_sum(z), 0.0)
    ss_ref[...] = jnp.where(carry, ss_ref[...], 0.0) + jnp.where(in_v, _lane_partial_sum(z * z), 0.0)

    @pl.when(jp == n_tiles - 1)
    def _():
        mu = jnp.sum(s_ref[...], axis=-1, keepdims=True) / v_width
        var = jnp.sum(ss_ref[...], axis=-1, keepdims=True) / v_width - mu * mu
        mu_ref[...] = mu
        rstd_ref[...] = lax.rsqrt(var + EPS)


def _gmlp_in(x, pre_g, w, w_out, *, bm, bn):
    h = x
    m = h.shape[0]
    n = w.shape[2]
    n_tiles = n // bn
    epilogue = functools.partial(_gmlp_in_epilogue, n_u_tiles=n_tiles // 2, n_tiles=n_tiles,
                                 v_width=n // 2)
    (z, mu, rstd), (w_out_bf,) = _up_proj(
        h, [(w, 0, 0)], n_col_tiles=n_tiles, bm=bm, bn=bn, weight_stationary=False, epilogue=epilogue,
        outs=[(jax.ShapeDtypeStruct((m, n), BF16), (bm, bn), lambda i, j: (i, j)),
              (jax.ShapeDtypeStruct((m, 1), F32), (bm, 1), lambda i, j: (i, 0)),
              (jax.ShapeDtypeStruct((m, 1), F32), (bm, 1), lambda i, j: (i, 0))],
        epi_scratch=[pltpu.VMEM((bm, LANES), F32), pltpu.VMEM((bm, LANES), F32)],
        sides=[_cast_rows_side(w_out, 0)], prenorm_gain=pre_g, name="gmlp_in")
    return z, mu, rstd, w_out_bf


def _causal_conv(z, prev, cw, cb):
    row = lax.broadcasted_iota(jnp.int32, z.shape, 0)
    z1 = jnp.where(row == 0, prev[7:8, :], pltpu.roll(z, 1, 0))
    z2 = jnp.where(row == 0, prev[6:7, :], jnp.where(row == 1, prev[7:8, :], pltpu.roll(z, 2, 0)))
    return cb + cw[0:1, :] * z2 + cw[1:2, :] * z1 + cw[2:3, :] * z


def _ffn_up_epilogue(ys, extras, outs, scratch, ip, jp, *, tiles_per_seq):
    cwg_ref, cwu_ref, cbg_ref, cbu_ref = extras
    seq_start = ip % tiles_per_seq == 0

    def branch(z, cw_ref, cb_ref, carry_ref):
        prev = jnp.where(seq_start, 0.0, carry_ref[...])
        out = _causal_conv(z, prev, cw_ref[...], cb_ref[...])
        carry_ref[...] = z[z.shape[0] - SUBLANES:, :]
        return out

    gate = branch(ys[0], cwg_ref, cbg_ref, scratch[0])
    up = branch(ys[1], cwu_ref, cbu_ref, scratch[1])
    outs[0][...] = (jax.nn.silu(gate) * up).astype(outs[0].dtype)


def _ffn_up(h, w, conv_w, conv_b, w_down, layer, *, seq, bm, tn, row_subtiles, more_sides=()):
    m = h.shape[0]
    f = w.shape[2] // 2
    nj = f // tn
    cb = conv_b.reshape(conv_b.shape[0], 1, 2 * f)
    (a,), side_outs = _up_proj(
        h, [(w, layer, 0), (w, layer, nj)], n_col_tiles=nj, bm=bm, bn=tn, weight_stationary=True,
        epilogue=functools.partial(_ffn_up_epilogue, tiles_per_seq=seq * row_subtiles // bm),
        extras=[(conv_w, (None, CONV_WIDTH, tn), lambda i, j: (layer, 0, j)),
                (conv_w, (None, CONV_WIDTH, tn), lambda i, j: (layer, 0, j + nj)),
                (cb, (None, 1, tn), lambda i, j: (layer, 0, j)),
                (cb, (None, 1, tn), lambda i, j: (layer, 0, j + nj))],
        outs=[(jax.ShapeDtypeStruct((m, f), BF16), (bm, tn), lambda i, j: (i, j))],
        epi_scratch=[pltpu.VMEM((SUBLANES, tn), F32), pltpu.VMEM((SUBLANES, tn), F32)],
        sides=[_cast_rows_side(w_down, layer), *more_sides], row_subtiles=row_subtiles, name="ffn_up")
    return (a, *side_outs)


def _rope_epilogue(ys, extras, outs, scratch, ip, jp, *, n_q_tiles, head_qk):
    cos = extras[0][...]
    sin = extras[1][...]
    y, o_ref = ys[0], outs[0]
    even = (lax.broadcasted_iota(jnp.int32, cos.shape, 1) & 1) == 0
    scale = jnp.where(jp >= n_q_tiles, head_qk ** -0.5, 1.0).astype(F32)
    for hh in range(y.shape[1] // head_qk):
        t = y[:, hh * head_qk:(hh + 1) * head_qk]
        nxt = pltpu.roll(t, head_qk - 1, 1)
        prv = pltpu.roll(t, 1, 1)
        r = (t * cos + jnp.where(even, -nxt, prv) * sin) * scale
        o_ref[hh] = r.astype(o_ref.dtype)


def _value_gate_epilogue(ys, extras, outs, scratch, ip, jp, *, n_v_tiles):
    o_ref = outs[0]
    head_v = o_ref.shape[2]
    y = ys[0]
    y = jnp.where(jp >= n_v_tiles, jax.nn.silu(y), y).astype(o_ref.dtype)
    for hh in range(o_ref.shape[0]):
        o_ref[hh] = y[:, hh * head_v:(hh + 1) * head_v]


def _ret_in(h, w, cos, sin, *, bm, bn, qk_dim, v_dim, head_qk):
    m = h.shape[0]
    nq = qk_dim // bn
    nv = v_dim // bn
    head_v = v_dim // RET_HEADS
    tab = (bm, head_qk)
    qk_heads = bn // head_qk
    vg_heads = bn // head_v
    (qk,), _ = _up_proj(
        h, [(w, 0, 0)], n_col_tiles=2 * nq, bm=bm, bn=bn, weight_stationary=True,
        epilogue=functools.partial(_rope_epilogue, n_q_tiles=nq, head_qk=head_qk),
        extras=[(cos, tab, lambda i, j: (i, 0)), (sin, tab, lambda i, j: (i, 0))],
        outs=[(jax.ShapeDtypeStruct((2 * RET_HEADS, m, head_qk), BF16), (qk_heads, bm, head_qk),
               lambda i, j: (j, i, 0))],
        name="ret_in_qk")
    (vg,), _ = _up_proj(
        h, [(w, 0, 2 * nq)], n_col_tiles=2 * nv, bm=bm, bn=bn, weight_stationary=True,
        epilogue=functools.partial(_value_gate_epilogue, n_v_tiles=nv),
        outs=[(jax.ShapeDtypeStruct((2 * RET_HEADS, m, head_v), BF16), (vg_heads, bm, head_v),
               lambda i, j: (j, i, 0))],
        name="ret_in_vg")
    return qk, vg


def _residual_norm_epilogue(acc, x_ref, pg_ref, ng_ref, xo_ref, ho_ref):
    xn = x_ref[...] + _rms(acc) * pg_ref[...]
    xo_ref[...] = xn
    if ho_ref is not None:
        ho_ref[...] = (_rms(xn) * ng_ref[...]).astype(ho_ref.dtype)


def _down_kernel(a_ref, w_ref, x_ref, pg_ref, *rest, with_next):
    if with_next:
        ng_ref, xo_ref, ho_ref, raw0_ref, raw1_ref = rest
    else:
        (xo_ref, raw0_ref, raw1_ref), ng_ref, ho_ref = rest, None, None
    raws = (raw0_ref, raw1_ref)

    def matmul(par):
        raws[par][...] = jnp.dot(a_ref[...], w_ref[...], preferred_element_type=F32)

    def finish(par):
        _residual_norm_epilogue(raws[par][...], x_ref, pg_ref, ng_ref, xo_ref, ho_ref)

    _staged_pipeline(pl.program_id(0), pl.num_programs(0) - 1, [matmul, finish])


def _down_proj(a, w, x, post_g, next_g=None, *, bm):
    m, kdim = a.shape
    d = w.shape[1]
    n = m // bm
    with_next = next_g is not None
    prev_row = pl.BlockSpec((bm, d), lambda s: (jnp.maximum(s - 1, 0), 0))
    gain = pl.BlockSpec((1, d), lambda s: (0, 0))
    gains = [post_g.reshape(1, d)] + ([next_g.reshape(1, d)] if with_next else [])
    out_shape = [jax.ShapeDtypeStruct((m, d), F32)]
    if with_next:
        out_shape.append(jax.ShapeDtypeStruct((m, d), BF16))
    res = pl.pallas_call(
        functools.partial(_down_kernel, with_next=with_next),
        grid=(n + 1,),
        in_specs=[
            pl.BlockSpec((bm, kdim), lambda s: (jnp.minimum(s, n - 1), 0)),
            pl.BlockSpec((kdim, d), lambda s: (0, 0)),
            prev_row,
        ] + [gain] * len(gains),
        out_specs=[prev_row] * len(out_shape),
        out_shape=out_shape,
        scratch_shapes=[pltpu.VMEM((bm, d), F32), pltpu.VMEM((bm, d), F32)],
        compiler_params=_params("arbitrary"),
        name="down_proj",
    )(a, w, x, *gains)
    return res if with_next else (res[0], None)


def _gmlp_down_kernel(u_ref, v_ref, mu_ref, rstd_ref, lg_ref, lb_ref, ws_ref, bs_ref, w_ref,
                      x_ref, pg_ref, ng_ref, xo_ref, ho_ref, raw0_ref, raw1_ref):
    raws = (raw0_ref, raw1_ref)
    bm, half = u_ref.shape
    gd = half // GMLP_GROUPS

    def gate_and_project(par):
        tril = (lax.broadcasted_iota(jnp.int32, (CHUNK, CHUNK), 0)
                >= lax.broadcasted_iota(jnp.int32, (CHUNK, CHUNK), 1))
        mu = mu_ref[...]
        rstd = rstd_ref[...]
        acc = None
        for g in range(GMLP_GROUPS):
            cols = slice(g * gd, (g + 1) * gd)
            ws = jnp.where(tril, ws_ref[g], 0.0).astype(BF16)
            bias = bs_ref[g]
            vn = (v_ref[:, cols].astype(F32) - mu) * rstd * lg_ref[:, cols] + lb_ref[:, cols]
            vn = vn.astype(BF16)
            gated = []
            for c in range(bm // CHUNK):
                rows = slice(c * CHUNK, (c + 1) * CHUNK)
                mixed = jnp.dot(ws, vn[rows], preferred_element_type=F32) + bias
                gated.append((u_ref[rows, cols].astype(F32) * mixed).astype(BF16))
            part = jnp.dot(jnp.concatenate(gated, axis=0), w_ref[cols, :], preferred_element_type=F32)
            acc = part if acc is None else acc + part
        raws[par][...] = acc

    def finish(par):
        _residual_norm_epilogue(raws[par][...], x_ref, pg_ref, ng_ref, xo_ref, ho_ref)

    _staged_pipeline(pl.program_id(0), pl.num_programs(0) - 1, [gate_and_project, finish])


def _gmlp_down(z, mu, rstd, ln_g, ln_b, w_s, b_s, w, x, post_g, next_g, *, bm):
    m = z.shape[0]
    half, d = w.shape
    n = m // bm

    def first(s):
        return jnp.minimum(s, n - 1)

    def last(s):
        return jnp.maximum(s - 1, 0)

    row = pl.BlockSpec((bm, d), lambda s: (last(s), 0))
    gain = pl.BlockSpec((1, d), lambda s: (0, 0))
    stat = pl.BlockSpec((bm, 1), lambda s: (first(s), 0))
    lnp = pl.BlockSpec((1, half), lambda s: (0, 0))
    return pl.pallas_call(
        _gmlp_down_kernel,
        grid=(n + 1,),
        in_specs=[
            pl.BlockSpec((bm, half), lambda s: (first(s), 0)),
            pl.BlockSpec((bm, half), lambda s: (first(s), 1)),
            stat, stat, lnp, lnp,
            pl.BlockSpec((GMLP_GROUPS, CHUNK, CHUNK), lambda s: (0, 0, 0)),
            pl.BlockSpec((GMLP_GROUPS, CHUNK, 1), lambda s: (0, 0, 0)),
            pl.BlockSpec((half, d), lambda s: (0, 0)),
            row, gain, gain,
        ],
        out_specs=[row, row],
        out_shape=[jax.ShapeDtypeStruct((m, d), F32), jax.ShapeDtypeStruct((m, d), BF16)],
        scratch_shapes=[pltpu.VMEM((bm, d), F32), pltpu.VMEM((bm, d), F32)],
        compiler_params=_params("arbitrary"),
        name="gmlp_down",
    )(z, z, mu, rstd, ln_g.reshape(1, half), ln_b.reshape(1, half), w_s,
      b_s.reshape(GMLP_GROUPS, CHUNK, 1), w, x, post_g.reshape(1, d), next_g.reshape(1, d))


def _rope_table_kernel(pos_ref, inv_ref, w_ref, cos_ref, sin_ref, wo_ref):
    ang = pos_ref[...].astype(F32) * inv_ref[...]
    cos_ref[...] = jnp.cos(ang)
    sin_ref[...] = jnp.sin(ang)
    wo_ref[...] = w_ref[...].astype(BF16)


def _rope_tables(pos, inv_freq, w, *, bm):
    m = pos.shape[0]
    dk = inv_freq.shape[1]
    n_steps = m // bm
    layers, kdim, n = w.shape
    wrows = kdim // n_steps
    assert wrows * n_steps == kdim and wrows % (2 * SUBLANES) == 0, (kdim, n_steps)
    tab = pl.BlockSpec((bm, dk), lambda i: (i, 0))
    wblk = pl.BlockSpec((layers, wrows, n), lambda i: (0, i, 0))
    return pl.pallas_call(
        _rope_table_kernel,
        grid=(n_steps,),
        in_specs=[pl.BlockSpec((bm, 1), lambda i: (i, 0)), pl.BlockSpec((1, dk), lambda i: (0, 0)), wblk],
        out_specs=[tab, tab, wblk],
        out_shape=[jax.ShapeDtypeStruct((m, dk), F32)] * 2 + [jax.ShapeDtypeStruct(w.shape, BF16)],
        compiler_params=_params("arbitrary"),
        name="rope_tables",
    )(pos, inv_freq, w)


def _ret_core_kernel(q_ref, k_ref, v_ref, g_ref, dm_ref, qd_ref, kd_ref, cd_ref, wsrc_ref,
                     o_ref, wdst_ref):
    wdst_ref[...] = wsrc_ref[...].astype(BF16)
    dmask = dm_ref[0]
    qdec = qd_ref[0]
    kdec = kd_ref[0]
    cdec = cd_ref[0]
    chunk = dmask.shape[0]
    state = jnp.zeros((q_ref.shape[1], v_ref.shape[1]), F32)
    for c in range(q_ref.shape[0] // chunk):
        rows = slice(c * chunk, (c + 1) * chunk)
        qc = q_ref[rows, :]
        kc = k_ref[rows, :]
        vc = v_ref[rows, :]
        scores = lax.dot_general(qc, kc, (((1,), (1,)), ((), ())), preferred_element_type=F32) * dmask
        intra = jnp.dot(scores.astype(BF16), vc, preferred_element_type=F32)
        qd = (qc.astype(F32) * qdec).astype(BF16)
        cross = jnp.dot(qd, state.astype(BF16), preferred_element_type=F32)
        kd = (kc.astype(F32) * kdec).astype(BF16)
        upd = lax.dot_general(kd, vc, (((0,), (0,)), ((), ())), preferred_element_type=F32)
        state = state * cdec + upd
        o = _rms(intra + cross)
        o_ref[rows, :] = (g_ref[rows, :].astype(F32) * o).astype(o_ref.dtype)


def _ret_core(qk, vg, dmask, qdec, kdec, cdec, w_out, *, batch, seq, stream_buffers):
    m, hq = qk.shape[1:]
    hv = vg.shape[2]
    chunk = dmask.shape[1]
    kdim, d = w_out.shape[1:]
    n_steps = batch * RET_HEADS
    wrows = kdim // n_steps
    assert wrows * n_steps == kdim and wrows % (2 * SUBLANES) == 0, (kdim, n_steps)
    deep = pl.Buffered(stream_buffers)
    in_specs = [
        pl.BlockSpec((None, seq, hq), lambda b, h: (h, b, 0), pipeline_mode=deep),
        pl.BlockSpec((None, seq, hq), lambda b, h: (RET_HEADS + h, b, 0), pipeline_mode=deep),
        pl.BlockSpec((None, seq, hv), lambda b, h: (h, b, 0), pipeline_mode=deep),
        pl.BlockSpec((None, seq, hv), lambda b, h: (RET_HEADS + h, b, 0), pipeline_mode=deep),
        pl.BlockSpec((1, chunk, chunk), lambda b, h: (h, 0, 0)),
        pl.BlockSpec((1, chunk, 1), lambda b, h: (h, 0, 0)),
        pl.BlockSpec((1, chunk, 1), lambda b, h: (h, 0, 0)),
        pl.BlockSpec((1, 1, 1), lambda b, h: (h, 0, 0)),
        pl.BlockSpec((None, wrows, d), lambda b, h: (0, b * RET_HEADS + h, 0)),
    ]
    out_specs = [pl.BlockSpec((seq, hv), lambda b, h: (b, h)),
                 pl.BlockSpec((wrows, d), lambda b, h: (b * RET_HEADS + h, 0))]

    def pipelined(*refs):
        pltpu.emit_pipeline(_ret_core_kernel, grid=(batch, RET_HEADS), in_specs=in_specs,
                            out_specs=out_specs)(*refs)

    in_hbm = pl.BlockSpec(memory_space=pl.ANY)
    return pl.pallas_call(
        pipelined,
        in_specs=[in_hbm] * len(in_specs),
        out_specs=[in_hbm] * len(out_specs),
        out_shape=[jax.ShapeDtypeStruct((m, hv * RET_HEADS), BF16),
                   jax.ShapeDtypeStruct((kdim, d), BF16)],
        compiler_params=pltpu.CompilerParams(vmem_limit_bytes=VMEM_LIMIT_BYTES),
        name="ret_core",
    )(qk, qk, vg, vg, dmask, qdec, kdec, cdec, w_out)


def _retention_constants(head_qk, chunk):
    log_gamma = jnp.log1p(-jnp.exp2(-5.0 - jnp.arange(RET_HEADS, dtype=F32)))
    idx = jnp.arange(chunk, dtype=F32)
    rel = idx[:, None] - idx[None, :]
    dmask = jnp.where(rel[None] >= 0,
                      jnp.exp(jnp.maximum(rel, 0.0)[None] * log_gamma[:, None, None]), 0.0)
    qdec = jnp.exp((idx[None, :] + 1.0) * log_gamma[:, None])[:, :, None]
    kdec = jnp.exp((chunk - 1.0 - idx)[None, :] * log_gamma[:, None])[:, :, None]
    cdec = jnp.exp(chunk * log_gamma)[:, None, None]
    inv_freq = 1.0 / (ROPE_BASE ** jnp.linspace(0.0, 1.0, head_qk // 2, dtype=F32))
    inv_freq = jnp.repeat(inv_freq, 2)[None, :]
    return dmask, qdec, kdec, cdec, inv_freq


def kernel(x, positions, mix_pre_g, mix_post_g, gmlp_w_in, gmlp_ln_g, gmlp_ln_b, gmlp_w_s, gmlp_b_s,
           gmlp_w_out, ret_w_in, ret_w_out, ffn_pre_g, ffn_post_g, ffn_w_up, ffn_conv_w, ffn_conv_b,
           ffn_w_down):
    batch, seq, d = x.shape
    m = batch * seq
    qk_dim = d
    v_dim = (ret_w_in.shape[2] - 2 * qk_dim) // 2
    head_qk = qk_dim // RET_HEADS
    xf = x.reshape(m, d)

    dmask, qdec, kdec, cdec, inv_freq = _retention_constants(head_qk, RET_CHUNK)
    cos, sin, gmlp_w_in_bf = _rope_tables(positions.reshape(m, 1), inv_freq, gmlp_w_in, bm=1024)

    z, mu, rstd, w_bf = _gmlp_in(xf, mix_pre_g[0], gmlp_w_in_bf, gmlp_w_out, bm=1024, bn=2048)
    xf, h = _gmlp_down(z, mu, rstd, gmlp_ln_g[0], gmlp_ln_b[0], gmlp_w_s[0], gmlp_b_s[0],
                       w_bf, xf, mix_post_g[0], ffn_pre_g[0], bm=256)
    ret_w_in_side = _cast_rows_side(ret_w_in, 0, n_used=64, keep_layer_axis=True)
    a, w_bf, ret_w_in_bf = _ffn_up(h, ffn_w_up, ffn_conv_w, ffn_conv_b, ffn_w_down, 0, seq=seq, bm=2048,
                                   tn=512, row_subtiles=2, more_sides=[ret_w_in_side])
    xf, h = _down_proj(a, w_bf, xf, ffn_post_g[0], mix_pre_g[1], bm=256)

    qk, vg = _ret_in(h, ret_w_in_bf, cos, sin, bm=1024, bn=2048,
                     qk_dim=qk_dim, v_dim=v_dim, head_qk=head_qk)
    o, w_bf = _ret_core(qk, vg, dmask, qdec, kdec, cdec, ret_w_out, batch=batch, seq=seq, stream_buffers=3)
    xf, h = _down_proj(o, w_bf, xf, mix_post_g[1], ffn_pre_g[1], bm=256)
    a, w_bf = _ffn_up(h, ffn_w_up, ffn_conv_w, ffn_conv_b, ffn_w_down, 1, seq=seq, bm=2048, tn=512,
                      row_subtiles=2)
    xf, _ = _down_proj(a, w_bf, xf, ffn_post_g[1], bm=256)
    return xf.reshape(batch, seq, d)
```

```python
import functools

import jax
import jax.numpy as jnp
import numpy as np
from jax import lax
from jax.experimental import pallas as pl
from jax.experimental.pallas import tpu as pltpu

EPS = 1e-6
CHUNK = 128
RET_CHUNK = 256
GMLP_GROUPS = 8
RET_HEADS = 8
ROPE_BASE = 10000.0
CONV_WIDTH = 3

F32 = jnp.float32
BF16 = jnp.bfloat16

V7X_VMEM_BYTES = 64 * 1024 * 1024
VMEM_LIMIT_BYTES = V7X_VMEM_BYTES - 8 * 1024 * 1024
LANES = 128
SUBLANES = 8


def _params(*semantics):
    return pltpu.CompilerParams(dimension_semantics=semantics, vmem_limit_bytes=VMEM_LIMIT_BYTES)


def _rms(x):
    return x * lax.rsqrt(jnp.mean(x * x, axis=-1, keepdims=True) + EPS)


def _staged_pipeline(s, n, stages):
    depth = len(stages)

    def run(active, step_parity):
        for k in active:
            stages[k]((step_parity - k) % 2)

    for fill in range(depth - 1):
        pl.when(s == fill)(functools.partial(run, range(fill + 1), fill % 2))
    for parity in (0, 1):
        pl.when((s >= depth - 1) & (s < n) & (s % 2 == parity))(
            functools.partial(run, range(depth), parity))
    for drain in range(1, depth):
        step = n - 1 + drain
        pl.when(s == step)(functools.partial(run, range(drain, depth), step % 2))


def _pipelined_call(kernel, *, n_steps, in_specs, out_specs, out_shape, scratch_shapes, name):
    n_scratch = len(scratch_shapes)

    def pipelined(*refs):
        operands, scratches = refs[:len(refs) - n_scratch], refs[len(refs) - n_scratch:]
        pltpu.emit_pipeline(kernel, grid=(n_steps,), in_specs=in_specs, out_specs=out_specs)(
            *operands, scratches=scratches)

    in_hbm = pl.BlockSpec(memory_space=pl.ANY)
    return pl.pallas_call(
        pipelined,
        in_specs=[in_hbm] * len(in_specs),
        out_specs=[in_hbm] * len(out_specs),
        out_shape=out_shape,
        scratch_shapes=scratch_shapes,
        compiler_params=pltpu.CompilerParams(vmem_limit_bytes=VMEM_LIMIT_BYTES),
        name=name,
    )


def _up_kernel(*refs, n_w, n_extra, n_out, side_jobs, weight_stationary, cast_weights, row_subtiles,
               prenorm, epilogue):
    h_ref = refs[0]
    if prenorm:
        gain_ref, refs = refs[1], refs[:1] + refs[2:]
    w_refs = refs[1:1 + n_w]
    pos = 1 + n_w
    extra_refs = refs[pos:pos + n_extra]
    pos += n_extra
    side_in_refs = []
    for n_in, _, _ in side_jobs:
        side_in_refs.append(refs[pos:pos + n_in])
        pos += n_in
    out_refs = refs[pos:pos + n_out]
    pos += n_out
    side_out_refs = []
    for _, n_o, _ in side_jobs:
        side_out_refs.append(refs[pos:pos + n_o])
        pos += n_o
    if cast_weights:
        wbf_refs = refs[pos:pos + n_w]
        pos += n_w
    else:
        wbf_refs = w_refs
    if prenorm:
        hn_ref = refs[pos]
        pos += 1
    epi_scratch = refs[pos:]

    if weight_stationary:
        j, i = pl.program_id(0), pl.program_id(1)
    else:
        i, j = pl.program_id(0), pl.program_id(1)

    if prenorm:
        @pl.when(j == 0)
        def _():
            hn_ref[...] = (_rms(h_ref[...]) * gain_ref[...]).astype(BF16)

        h_ref = hn_ref

    if cast_weights:
        @pl.when(i == 0)
        def _():
            for w_ref, wbf_ref in zip(w_refs, wbf_refs):
                wbf_ref[...] = w_ref[...].astype(BF16)

    if epi_scratch:
        @pl.when((i == 0) & (j == 0))
        def _():
            for ref in epi_scratch:
                ref[...] = jnp.zeros_like(ref)

    if row_subtiles == 1:
        h = h_ref[...]
        ys = [jnp.dot(h, wbf_ref[...], preferred_element_type=F32) for wbf_ref in wbf_refs]
        epilogue(ys, extra_refs, out_refs, epi_scratch, i, j)
    else:
        sub = h_ref.shape[0] // row_subtiles

        def row_tile(r, carry):
            rows = pl.ds(pl.multiple_of(r * sub, sub), sub)
            h = h_ref[rows, :]
            ys = [jnp.dot(h, wbf_ref[...], preferred_element_type=F32) for wbf_ref in wbf_refs]
            epilogue(ys, extra_refs, [o.at[rows, :] for o in out_refs], epi_scratch,
                     i * row_subtiles + r, j)
            return carry

        lax.fori_loop(0, row_subtiles, row_tile, 0)
    for (_, _, fn), ins, outs in zip(side_jobs, side_in_refs, side_out_refs):
        fn(ins, outs)


def _up_proj(h, weights, *, n_col_tiles, bm, bn, weight_stationary, epilogue, extras=(), outs,
             epi_scratch=(), sides=(), row_subtiles=1, prenorm_gain=None, name):
    m, d = h.shape
    n_i, n_j = m // bm, n_col_tiles
    cast_weights = weights[0][0].dtype != BF16
    prenorm = prenorm_gain is not None
    assert weight_stationary or not cast_weights
    assert not (prenorm and weight_stationary)
    if weight_stationary:
        grid = (n_j, n_i)

        def ij(fn):
            return lambda j, i: fn(i, j)

        def step(fn):
            return lambda j, i: fn(j * n_i + i)
    else:
        grid = (n_i, n_j)

        def ij(fn):
            return fn

        def step(fn):
            return lambda i, j: fn(i * n_j + j)

    built = [side(n_i * n_j) for side in sides]
    side_ins = [entry for ins, _, _ in built for entry in ins]
    side_outs = [entry for _, outs_, _ in built for entry in outs_]
    in_specs = [pl.BlockSpec((bm, d), ij(lambda i, j: (i, 0)))]
    if prenorm:
        in_specs.append(pl.BlockSpec((1, d), ij(lambda i, j: (0, 0))))
    for _, layer, c0 in weights:
        in_specs.append(pl.BlockSpec((None, d, bn), ij(lambda i, j, layer=layer, c0=c0: (layer, 0, j + c0))))
    in_specs += [pl.BlockSpec(shape, ij(fn)) for _, shape, fn in extras]
    in_specs += [pl.BlockSpec(shape, step(fn)) for _, shape, fn in side_ins]
    out_specs = [pl.BlockSpec(shape, ij(fn)) for _, shape, fn in outs]
    out_specs += [pl.BlockSpec(shape, step(fn)) for _, shape, fn in side_outs]
    scratch = []
    if cast_weights:
        scratch += [pltpu.VMEM((d, bn), BF16) for _ in weights]
    if prenorm:
        scratch.append(pltpu.VMEM((bm, d), BF16))
    scratch += list(epi_scratch)
    kern = functools.partial(_up_kernel, n_w=len(weights), n_extra=len(extras), n_out=len(outs),
                             side_jobs=[(len(ins), len(outs_), fn) for ins, outs_, fn in built],
                             weight_stationary=weight_stationary, cast_weights=cast_weights,
                             row_subtiles=row_subtiles, prenorm=prenorm, epilogue=epilogue)
    res = pl.pallas_call(
        kern,
        grid=grid,
        in_specs=in_specs,
        out_specs=out_specs,
        out_shape=[sds for sds, _, _ in outs] + [sds for sds, _, _ in side_outs],
        scratch_shapes=scratch,
        compiler_params=_params("arbitrary", "arbitrary"),
        name=name,
    )(h, *([prenorm_gain.reshape(1, d)] if prenorm else []), *[w for w, _, _ in weights],
      *[a for a, _, _ in extras], *[a for a, _, _ in side_ins])
    return res[:len(outs)], res[len(outs):]


def _cast_rows_side(w, layer, *, n_used=None, keep_layer_axis=False):
    kdim, d = w.shape[1:]

    def build(n_steps):
        used = n_steps if n_used is None else n_used
        rows = kdim // used
        assert used <= n_steps and rows * used == kdim and rows % (2 * SUBLANES) == 0, (kdim, used)

        def fn(in_refs, out_refs):
            out_refs[0][...] = in_refs[0][...].astype(BF16)

        def block(t):
            return jnp.minimum(t, used - 1)

        if keep_layer_axis:
            out = (jax.ShapeDtypeStruct((1, kdim, d), BF16), (None, rows, d), lambda t: (0, block(t), 0))
        else:
            out = (jax.ShapeDtypeStruct((kdim, d), BF16), (rows, d), lambda t: (block(t), 0))
        return [(w, (None, rows, d), lambda t: (layer, block(t), 0))], [out], fn

    return build


def _lane_partial_sum(y):
    acc = y[:, 0:LANES]
    for c in range(1, y.shape[1] // LANES):
        acc = acc + y[:, c * LANES:(c + 1) * LANES]
    return acc


def _gelu_exact(y):
    return 0.5 * y * (1.0 + lax.erf(y * np.float32(np.sqrt(0.5))))


def _gmlp_in_epilogue(ys, extras, outs, scratch, ip, jp, *, n_u_tiles, n_tiles, v_width):
    z_ref, mu_ref, rstd_ref = outs
    s_ref, ss_ref = scratch
    z = _gelu_exact(ys[0])
    z_ref[...] = z.astype(z_ref.dtype)
    carry = jp > n_u_tiles
    in_v = jp >= n_u_tiles
    s_ref[...] = jnp.where(carry, s_ref[...], 0.0) + jnp.where(in_v, _lane_partial_sum(z), 0.0)
    ss_ref[...] = jnp.where(carry, ss_ref[...], 0.0) + jnp.where(in_v, _lane_partial_sum(z * z), 0.0)

    @pl.when(jp == n_tiles - 1)
    def _():
        mu = jnp.sum(s_ref[...], axis=-1, keepdims=True) / v_width
        var = jnp.sum(ss_ref[...], axis=-1, keepdims=True) / v_width - mu * mu
        mu_ref[...] = mu
        rstd_ref[...] = lax.rsqrt(var + EPS)


def _gmlp_in(x, pre_g, w, w_out, *, bm, bn):
    h = x
    m = h.shape[0]
    n = w.shape[2]
    n_tiles = n // bn
    epilogue = functools.partial(_gmlp_in_epilogue, n_u_tiles=n_tiles // 2, n_tiles=n_tiles,
                                 v_width=n // 2)
    (z, mu, rstd), (w_out_bf,) = _up_proj(
        h, [(w, 0, 0)], n_col_tiles=n_tiles, bm=bm, bn=bn, weight_stationary=False, epilogue=epilogue,
        outs=[(jax.ShapeDtypeStruct((m, n), BF16), (bm, bn), lambda i, j: (i, j)),
              (jax.ShapeDtypeStruct((m, 1), F32), (bm, 1), lambda i, j: (i, 0)),
              (jax.ShapeDtypeStruct((m, 1), F32), (bm, 1), lambda i, j: (i, 0))],
        epi_scratch=[pltpu.VMEM((bm, LANES), F32), pltpu.VMEM((bm, LANES), F32)],
        sides=[_cast_rows_side(w_out, 0)], prenorm_gain=pre_g, name="gmlp_in")
    return z, mu, rstd, w_out_bf


def _causal_conv(z, prev, cw, cb):
    row = lax.broadcasted_iota(jnp.int32, z.shape, 0)
    z1 = jnp.where(row == 0, prev[7:8, :], pltpu.roll(z, 1, 0))
    z2 = jnp.where(row == 0, prev[6:7, :], jnp.where(row == 1, prev[7:8, :], pltpu.roll(z, 2, 0)))
    return cb + cw[0:1, :] * z2 + cw[1:2, :] * z1 + cw[2:3, :] * z


def _ffn_up_epilogue(ys, extras, outs, scratch, ip, jp, *, tiles_per_seq):
    cwg_ref, cwu_ref, cbg_ref, cbu_ref = extras
    seq_start = ip % tiles_per_seq == 0

    def branch(z, cw_ref, cb_ref, carry_ref):
        prev = jnp.where(seq_start, 0.0, carry_ref[...])
        out = _causal_conv(z, prev, cw_ref[...], cb_ref[...])
        carry_ref[...] = z[z.shape[0] - SUBLANES:, :]
        return out

    gate = branch(ys[0], cwg_ref, cbg_ref, scratch[0])
    up = branch(ys[1], cwu_ref, cbu_ref, scratch[1])
    outs[0][...] = (jax.nn.silu(gate) * up).astype(outs[0].dtype)


def _ffn_up(h, w, conv_w, conv_b, w_down, layer, *, seq, bm, tn, row_subtiles, more_sides=()):
    m = h.shape[0]
    f = w.shape[2] // 2
    nj = f // tn
    cb = conv_b.reshape(conv_b.shape[0], 1, 2 * f)
    (a,), side_outs = _up_proj(
        h, [(w, layer, 0), (w, layer, nj)], n_col_tiles=nj, bm=bm, bn=tn, weight_stationary=True,
        epilogue=functools.partial(_ffn_up_epilogue, tiles_per_seq=seq * row_subtiles // bm),
        extras=[(conv_w, (None, CONV_WIDTH, tn), lambda i, j: (layer, 0, j)),
                (conv_w, (None, CONV_WIDTH, tn), lambda i, j: (layer, 0, j + nj)),
                (cb, (None, 1, tn), lambda i, j: (layer, 0, j)),
                (cb, (None, 1, tn), lambda i, j: (layer, 0, j + nj))],
        outs=[(jax.ShapeDtypeStruct((m, f), BF16), (bm, tn), lambda i, j: (i, j))],
        epi_scratch=[pltpu.VMEM((SUBLANES, tn), F32), pltpu.VMEM((SUBLANES, tn), F32)],
        sides=[_cast_rows_side(w_down, layer), *more_sides], row_subtiles=row_subtiles, name="ffn_up")
    return (a, *side_outs)


def _rope_epilogue(ys, extras, outs, scratch, ip, jp, *, n_q_tiles, head_qk):
    cos = extras[0][...]
    sin = extras[1][...]
    y, o_ref = ys[0], outs[0]
    even = (lax.broadcasted_iota(jnp.int32, cos.shape, 1) & 1) == 0
    scale = jnp.where(jp >= n_q_tiles, head_qk ** -0.5, 1.0).astype(F32)
    for hh in range(y.shape[1] // head_qk):
        t = y[:, hh * head_qk:(hh + 1) * head_qk]
        nxt = pltpu.roll(t, head_qk - 1, 1)
        prv = pltpu.roll(t, 1, 1)
        r = (t * cos + jnp.where(even, -nxt, prv) * sin) * scale
        o_ref[hh] = r.astype(o_ref.dtype)


def _value_gate_epilogue(ys, extras, outs, scratch, ip, jp, *, n_v_tiles):
    o_ref = outs[0]
    head_v = o_ref.shape[2]
    y = ys[0]
    y = jnp.where(jp >= n_v_tiles, jax.nn.silu(y), y).astype(o_ref.dtype)
    for hh in range(o_ref.shape[0]):
        o_ref[hh] = y[:, hh * head_v:(hh + 1) * head_v]


def _ret_in(h, w, cos, sin, *, bm, bn, qk_dim, v_dim, head_qk):
    m = h.shape[0]
    nq = qk_dim // bn
    nv = v_dim // bn
    head_v = v_dim // RET_HEADS
    tab = (bm, head_qk)
    qk_heads = bn // head_qk
    vg_heads = bn // head_v
    (qk,), _ = _up_proj(
        h, [(w, 0, 0)], n_col_tiles=2 * nq, bm=bm, bn=bn, weight_stationary=True,
        epilogue=functools.partial(_rope_epilogue, n_q_tiles=nq, head_qk=head_qk),
        extras=[(cos, tab, lambda i, j: (i, 0)), (sin, tab, lambda i, j: (i, 0))],
        outs=[(jax.ShapeDtypeStruct((2 * RET_HEADS, m, head_qk), BF16), (qk_heads, bm, head_qk),
               lambda i, j: (j, i, 0))],
        name="ret_in_qk")
    (vg,), _ = _up_proj(
        h, [(w, 0, 2 * nq)], n_col_tiles=2 * nv, bm=bm, bn=bn, weight_stationary=True,
        epilogue=functools.partial(_value_gate_epilogue, n_v_tiles=nv),
        outs=[(jax.ShapeDtypeStruct((2 * RET_HEADS, m, head_v), BF16), (vg_heads, bm, head_v),
               lambda i, j: (j, i, 0))],
        name="ret_in_vg")
    return qk, vg


def _residual_norm_epilogue(acc, x_ref, pg_ref, ng_ref, xo_ref, ho_ref):
    xn = x_ref[...] + _rms(acc) * pg_ref[...]
    xo_ref[...] = xn
    if ho_ref is not None:
        ho_ref[...] = (_rms(xn) * ng_ref[...]).astype(ho_ref.dtype)


def _down_kernel(a_ref, w_ref, x_ref, pg_ref, *rest, with_next):
    if with_next:
        ng_ref, xo_ref, ho_ref, raw0_ref, raw1_ref = rest
    else:
        (xo_ref, raw0_ref, raw1_ref), ng_ref, ho_ref = rest, None, None
    raws = (raw0_ref, raw1_ref)

    def matmul(par):
        raws[par][...] = jnp.dot(a_ref[...], w_ref[...], preferred_element_type=F32)

    def finish(par):
        _residual_norm_epilogue(raws[par][...], x_ref, pg_ref, ng_ref, xo_ref, ho_ref)

    _staged_pipeline(pl.program_id(0), pl.num_programs(0) - 1, [matmul, finish])


def _down_proj(a, w, x, post_g, next_g=None, *, bm):
    m, kdim = a.shape
    d = w.shape[1]
    n = m // bm
    with_next = next_g is not None
    prev_row = pl.BlockSpec((bm, d), lambda s: (jnp.maximum(s - 1, 0), 0))
    gain = pl.BlockSpec((1, d), lambda s: (0, 0))
    gains = [post_g.reshape(1, d)] + ([next_g.reshape(1, d)] if with_next else [])
    out_shape = [jax.ShapeDtypeStruct((m, d), F32)]
    if with_next:
        out_shape.append(jax.ShapeDtypeStruct((m, d), BF16))
    res = _pipelined_call(
        functools.partial(_down_kernel, with_next=with_next),
        n_steps=n + 1,
        in_specs=[
            pl.BlockSpec((bm, kdim), lambda s: (jnp.minimum(s, n - 1), 0)),
            pl.BlockSpec((kdim, d), lambda s: (0, 0)),
            prev_row,
        ] + [gain] * len(gains),
        out_specs=[prev_row] * len(out_shape),
        out_shape=out_shape,
        scratch_shapes=[pltpu.VMEM((bm, d), F32), pltpu.VMEM((bm, d), F32)],
        name="down_proj",
    )(a, w, x, *gains)
    return res if with_next else (res[0], None)


def _gmlp_down_kernel(u_ref, v_ref, mu_ref, rstd_ref, lg_ref, lb_ref, ws_ref, bs_ref, w_ref,
                      x_ref, pg_ref, ng_ref, xo_ref, ho_ref, raw0_ref, raw1_ref):
    raws = (raw0_ref, raw1_ref)
    bm, half = u_ref.shape
    gd = half // GMLP_GROUPS

    def gate_and_project(par):
        tril = (lax.broadcasted_iota(jnp.int32, (CHUNK, CHUNK), 0)
                >= lax.broadcasted_iota(jnp.int32, (CHUNK, CHUNK), 1))
        mu = mu_ref[...]
        rstd = rstd_ref[...]
        acc = None
        for g in range(GMLP_GROUPS):
            cols = slice(g * gd, (g + 1) * gd)
            ws = jnp.where(tril, ws_ref[g], 0.0).astype(BF16)
            bias = bs_ref[g]
            vn = (v_ref[:, cols].astype(F32) - mu) * rstd * lg_ref[:, cols] + lb_ref[:, cols]
            vn = vn.astype(BF16)
            gated = []
            for c in range(bm // CHUNK):
                rows = slice(c * CHUNK, (c + 1) * CHUNK)
                mixed = jnp.dot(ws, vn[rows], preferred_element_type=F32) + bias
                gated.append((u_ref[rows, cols].astype(F32) * mixed).astype(BF16))
            part = jnp.dot(jnp.concatenate(gated, axis=0), w_ref[cols, :], preferred_element_type=F32)
            acc = part if acc is None else acc + part
        raws[par][...] = acc

    def finish(par):
        _residual_norm_epilogue(raws[par][...], x_ref, pg_ref, ng_ref, xo_ref, ho_ref)

    _staged_pipeline(pl.program_id(0), pl.num_programs(0) - 1, [gate_and_project, finish])


def _gmlp_down(z, mu, rstd, ln_g, ln_b, w_s, b_s, w, x, post_g, next_g, *, bm):
    m = z.shape[0]
    half, d = w.shape
    n = m // bm

    def first(s):
        return jnp.minimum(s, n - 1)

    def last(s):
        return jnp.maximum(s - 1, 0)

    row = pl.BlockSpec((bm, d), lambda s: (last(s), 0))
    gain = pl.BlockSpec((1, d), lambda s: (0, 0))
    stat = pl.BlockSpec((bm, 1), lambda s: (first(s), 0))
    lnp = pl.BlockSpec((1, half), lambda s: (0, 0))
    return _pipelined_call(
        _gmlp_down_kernel,
        n_steps=n + 1,
        in_specs=[
            pl.BlockSpec((bm, half), lambda s: (first(s), 0)),
            pl.BlockSpec((bm, half), lambda s: (first(s), 1)),
            stat, stat, lnp, lnp,
            pl.BlockSpec((GMLP_GROUPS, CHUNK, CHUNK), lambda s: (0, 0, 0)),
            pl.BlockSpec((GMLP_GROUPS, CHUNK, 1), lambda s: (0, 0, 0)),
            pl.BlockSpec((half, d), lambda s: (0, 0)),
            row, gain, gain,
        ],
        out_specs=[row, row],
        out_shape=[jax.ShapeDtypeStruct((m, d), F32), jax.ShapeDtypeStruct((m, d), BF16)],
        scratch_shapes=[pltpu.VMEM((bm, d), F32), pltpu.VMEM((bm, d), F32)],
        name="gmlp_down",
    )(z, z, mu, rstd, ln_g.reshape(1, half), ln_b.reshape(1, half), w_s,
      b_s.reshape(GMLP_GROUPS, CHUNK, 1), w, x, post_g.reshape(1, d), next_g.reshape(1, d))


def _rope_table_kernel(pos_ref, inv_ref, w_ref, cos_ref, sin_ref, wo_ref):
    ang = pos_ref[...].astype(F32) * inv_ref[...]
    cos_ref[...] = jnp.cos(ang)
    sin_ref[...] = jnp.sin(ang)
    wo_ref[...] = w_ref[...].astype(BF16)


def _rope_tables(pos, inv_freq, w, *, bm):
    m = pos.shape[0]
    dk = inv_freq.shape[1]
    n_steps = m // bm
    layers, kdim, n = w.shape
    wrows = kdim // n_steps
    assert wrows * n_steps == kdim and wrows % (2 * SUBLANES) == 0, (kdim, n_steps)
    tab = pl.BlockSpec((bm, dk), lambda i: (i, 0))
    wblk = pl.BlockSpec((layers, wrows, n), lambda i: (0, i, 0))
    return pl.pallas_call(
        _rope_table_kernel,
        grid=(n_steps,),
        in_specs=[pl.BlockSpec((bm, 1), lambda i: (i, 0)), pl.BlockSpec((1, dk), lambda i: (0, 0)), wblk],
        out_specs=[tab, tab, wblk],
        out_shape=[jax.ShapeDtypeStruct((m, dk), F32)] * 2 + [jax.ShapeDtypeStruct(w.shape, BF16)],
        compiler_params=_params("arbitrary"),
        name="rope_tables",
    )(pos, inv_freq, w)


def _ret_core_kernel(q_ref, k_ref, v_ref, g_ref, dm_ref, qd_ref, kd_ref, cd_ref, wsrc_ref,
                     o_ref, wdst_ref):
    wdst_ref[...] = wsrc_ref[...].astype(BF16)
    dmask = dm_ref[0]
    qdec = qd_ref[0]
    kdec = kd_ref[0]
    cdec = cd_ref[0]
    chunk = dmask.shape[0]
    state = jnp.zeros((q_ref.shape[1], v_ref.shape[1]), F32)
    for c in range(q_ref.shape[0] // chunk):
        rows = slice(c * chunk, (c + 1) * chunk)
        qc = q_ref[rows, :]
        kc = k_ref[rows, :]
        vc = v_ref[rows, :]
        scores = lax.dot_general(qc, kc, (((1,), (1,)), ((), ())), preferred_element_type=F32) * dmask
        intra = jnp.dot(scores.astype(BF16), vc, preferred_element_type=F32)
        qd = (qc.astype(F32) * qdec).astype(BF16)
        cross = jnp.dot(qd, state.astype(BF16), preferred_element_type=F32)
        kd = (kc.astype(F32) * kdec).astype(BF16)
        upd = lax.dot_general(kd, vc, (((0,), (0,)), ((), ())), preferred_element_type=F32)
        state = state * cdec + upd
        o = _rms(intra + cross)
        o_ref[rows, :] = (g_ref[rows, :].astype(F32) * o).astype(o_ref.dtype)


def _ret_core(qk, vg, dmask, qdec, kdec, cdec, w_out, *, batch, seq, stream_buffers):
    m, hq = qk.shape[1:]
    hv = vg.shape[2]
    chunk = dmask.shape[1]
    kdim, d = w_out.shape[1:]
    n_steps = batch * RET_HEADS
    wrows = kdim // n_steps
    assert wrows * n_steps == kdim and wrows % (2 * SUBLANES) == 0, (kdim, n_steps)
    deep = pl.Buffered(stream_buffers)
    in_specs = [
        pl.BlockSpec((None, seq, hq), lambda b, h: (h, b, 0), pipeline_mode=deep),
        pl.BlockSpec((None, seq, hq), lambda b, h: (RET_HEADS + h, b, 0), pipeline_mode=deep),
        pl.BlockSpec((None, seq, hv), lambda b, h: (h, b, 0), pipeline_mode=deep),
        pl.BlockSpec((None, seq, hv), lambda b, h: (RET_HEADS + h, b, 0), pipeline_mode=deep),
        pl.BlockSpec((1, chunk, chunk), lambda b, h: (h, 0, 0)),
        pl.BlockSpec((1, chunk, 1), lambda b, h: (h, 0, 0)),
        pl.BlockSpec((1, chunk, 1), lambda b, h: (h, 0, 0)),
        pl.BlockSpec((1, 1, 1), lambda b, h: (h, 0, 0)),
        pl.BlockSpec((None, wrows, d), lambda b, h: (0, b * RET_HEADS + h, 0)),
    ]
    out_specs = [pl.BlockSpec((seq, hv), lambda b, h: (b, h)),
                 pl.BlockSpec((wrows, d), lambda b, h: (b * RET_HEADS + h, 0))]

    def pipelined(*refs):
        pltpu.emit_pipeline(_ret_core_kernel, grid=(batch, RET_HEADS), in_specs=in_specs,
                            out_specs=out_specs)(*refs)

    in_hbm = pl.BlockSpec(memory_space=pl.ANY)
    return pl.pallas_call(
        pipelined,
        in_specs=[in_hbm] * len(in_specs),
        out_specs=[in_hbm] * len(out_specs),
        out_shape=[jax.ShapeDtypeStruct((m, hv * RET_HEADS), BF16),
                   jax.ShapeDtypeStruct((kdim, d), BF16)],
        compiler_params=pltpu.CompilerParams(vmem_limit_bytes=VMEM_LIMIT_BYTES),
        name="ret_core",
    )(qk, qk, vg, vg, dmask, qdec, kdec, cdec, w_out)


def _retention_constants(head_qk, chunk):
    log_gamma = jnp.log1p(-jnp.exp2(-5.0 - jnp.arange(RET_HEADS, dtype=F32)))
    idx = jnp.arange(chunk, dtype=F32)
    rel = idx[:, None] - idx[None, :]
    dmask = jnp.where(rel[None] >= 0,
                      jnp.exp(jnp.maximum(rel, 0.0)[None] * log_gamma[:, None, None]), 0.0)
    qdec = jnp.exp((idx[None, :] + 1.0) * log_gamma[:, None])[:, :, None]
    kdec = jnp.exp((chunk - 1.0 - idx)[None, :] * log_gamma[:, None])[:, :, None]
    cdec = jnp.exp(chunk * log_gamma)[:, None, None]
    inv_freq = 1.0 / (ROPE_BASE ** jnp.linspace(0.0, 1.0, head_qk // 2, dtype=F32))
    inv_freq = jnp.repeat(inv_freq, 2)[None, :]
    return dmask, qdec, kdec, cdec, inv_freq


def kernel(x, positions, mix_pre_g, mix_post_g, gmlp_w_in, gmlp_ln_g, gmlp_ln_b, gmlp_w_s, gmlp_b_s,
           gmlp_w_out, ret_w_in, ret_w_out, ffn_pre_g, ffn_post_g, ffn_w_up, ffn_conv_w, ffn_conv_b,
           ffn_w_down):
    batch, seq, d = x.shape
    m = batch * seq
    qk_dim = d
    v_dim = (ret_w_in.shape[2] - 2 * qk_dim) // 2
    head_qk = qk_dim // RET_HEADS
    xf = x.reshape(m, d)

    dmask, qdec, kdec, cdec, inv_freq = _retention_constants(head_qk, RET_CHUNK)
    cos, sin, gmlp_w_in_bf = _rope_tables(positions.reshape(m, 1), inv_freq, gmlp_w_in, bm=1024)

    z, mu, rstd, w_bf = _gmlp_in(xf, mix_pre_g[0], gmlp_w_in_bf, gmlp_w_out, bm=1024, bn=2048)
    xf, h = _gmlp_down(z, mu, rstd, gmlp_ln_g[0], gmlp_ln_b[0], gmlp_w_s[0], gmlp_b_s[0],
                       w_bf, xf, mix_post_g[0], ffn_pre_g[0], bm=256)
    ret_w_in_side = _cast_rows_side(ret_w_in, 0, n_used=64, keep_layer_axis=True)
    a, w_bf, ret_w_in_bf = _ffn_up(h, ffn_w_up, ffn_conv_w, ffn_conv_b, ffn_w_down, 0, seq=seq, bm=2048,
                                   tn=512, row_subtiles=2, more_sides=[ret_w_in_side])
    xf, h = _down_proj(a, w_bf, xf, ffn_post_g[0], mix_pre_g[1], bm=256)

    qk, vg = _ret_in(h, ret_w_in_bf, cos, sin, bm=1024, bn=2048,
                     qk_dim=qk_dim, v_dim=v_dim, head_qk=head_qk)
    o, w_bf = _ret_core(qk, vg, dmask, qdec, kdec, cdec, ret_w_out, batch=batch, seq=seq, stream_buffers=3)
    xf, h = _down_proj(o, w_bf, xf, mix_post_g[1], ffn_pre_g[1], bm=256)
    a, w_bf = _ffn_up(h, ffn_w_up, ffn_conv_w, ffn_conv_b, ffn_w_down, 1, seq=seq, bm=2048, tn=512,
                      row_subtiles=2)
    xf, _ = _down_proj(a, w_bf, xf, ffn_post_g[1], bm=256)
    return xf.reshape(batch, seq, d)
```

```python
import functools

import jax
import jax.numpy as jnp
import numpy as np
from jax import lax
from jax.experimental import pallas as pl
from jax.experimental.pallas import tpu as pltpu

EPS = 1e-6
CHUNK = 128
RET_CHUNK = 256
GMLP_GROUPS = 8
RET_HEADS = 8
ROPE_BASE = 10000.0
CONV_WIDTH = 3

F32 = jnp.float32
BF16 = jnp.bfloat16

V7X_VMEM_BYTES = 64 * 1024 * 1024
VMEM_LIMIT_BYTES = V7X_VMEM_BYTES - 8 * 1024 * 1024
LANES = 128
SUBLANES = 8


def _params(*semantics):
    return pltpu.CompilerParams(dimension_semantics=semantics, vmem_limit_bytes=VMEM_LIMIT_BYTES)


def _rms(x):
    return x * lax.rsqrt(jnp.mean(x * x, axis=-1, keepdims=True) + EPS)


def _staged_pipeline(s, n, stages):
    depth = len(stages)

    def run(active, step_parity):
        for k in active:
            stages[k]((step_parity - k) % 2)

    for fill in range(depth - 1):
        pl.when(s == fill)(functools.partial(run, range(fill + 1), fill % 2))
    for parity in (0, 1):
        pl.when((s >= depth - 1) & (s < n) & (s % 2 == parity))(
            functools.partial(run, range(depth), parity))
    for drain in range(1, depth):
        step = n - 1 + drain
        pl.when(s == step)(functools.partial(run, range(drain, depth), step % 2))


def _up_kernel(*refs, n_w, n_extra, n_out, side_jobs, weight_stationary, cast_weights, row_subtiles,
               prenorm, epilogue):
    h_ref = refs[0]
    if prenorm:
        gain_ref, refs = refs[1], refs[:1] + refs[2:]
    w_refs = refs[1:1 + n_w]
    pos = 1 + n_w
    extra_refs = refs[pos:pos + n_extra]
    pos += n_extra
    side_in_refs = []
    for n_in, _, _ in side_jobs:
        side_in_refs.append(refs[pos:pos + n_in])
        pos += n_in
    out_refs = refs[pos:pos + n_out]
    pos += n_out
    side_out_refs = []
    for _, n_o, _ in side_jobs:
        side_out_refs.append(refs[pos:pos + n_o])
        pos += n_o
    if cast_weights:
        wbf_refs = refs[pos:pos + n_w]
        pos += n_w
    else:
        wbf_refs = w_refs
    if prenorm:
        hn_ref = refs[pos]
        pos += 1
    epi_scratch = refs[pos:]

    if weight_stationary:
        j, i = pl.program_id(0), pl.program_id(1)
    else:
        i, j = pl.program_id(0), pl.program_id(1)

    if prenorm:
        @pl.when(j == 0)
        def _():
            hn_ref[...] = (_rms(h_ref[...]) * gain_ref[...]).astype(BF16)

        h_ref = hn_ref

    if cast_weights:
        @pl.when(i == 0)
        def _():
            for w_ref, wbf_ref in zip(w_refs, wbf_refs):
                wbf_ref[...] = w_ref[...].astype(BF16)

    if epi_scratch:
        @pl.when((i == 0) & (j == 0))
        def _():
            for ref in epi_scratch:
                ref[...] = jnp.zeros_like(ref)

    if row_subtiles == 1:
        h = h_ref[...]
        ys = [jnp.dot(h, wbf_ref[...], preferred_element_type=F32) for wbf_ref in wbf_refs]
        epilogue(ys, extra_refs, out_refs, epi_scratch, i, j)
    else:
        sub = h_ref.shape[0] // row_subtiles

        def row_tile(r, carry):
            rows = pl.ds(pl.multiple_of(r * sub, sub), sub)
            h = h_ref[rows, :]
            ys = [jnp.dot(h, wbf_ref[...], preferred_element_type=F32) for wbf_ref in wbf_refs]
            epilogue(ys, extra_refs, [o.at[rows, :] for o in out_refs], epi_scratch,
                     i * row_subtiles + r, j)
            return carry

        lax.fori_loop(0, row_subtiles, row_tile, 0)
    for (_, _, fn), ins, outs in zip(side_jobs, side_in_refs, side_out_refs):
        fn(ins, outs)


def _up_proj(h, weights, *, n_col_tiles, bm, bn, weight_stationary, epilogue, extras=(), outs,
             epi_scratch=(), sides=(), row_subtiles=1, prenorm_gain=None, name):
    m, d = h.shape
    n_i, n_j = m // bm, n_col_tiles
    cast_weights = weights[0][0].dtype != BF16
    prenorm = prenorm_gain is not None
    assert weight_stationary or not cast_weights
    assert not (prenorm and weight_stationary)
    if weight_stationary:
        grid = (n_j, n_i)

        def ij(fn):
            return lambda j, i: fn(i, j)

        def step(fn):
            return lambda j, i: fn(j * n_i + i)
    else:
        grid = (n_i, n_j)

        def ij(fn):
            return fn

        def step(fn):
            return lambda i, j: fn(i * n_j + j)

    built = [side(n_i * n_j) for side in sides]
    side_ins = [entry for ins, _, _ in built for entry in ins]
    side_outs = [entry for _, outs_, _ in built for entry in outs_]
    in_specs = [pl.BlockSpec((bm, d), ij(lambda i, j: (i, 0)))]
    if prenorm:
        in_specs.append(pl.BlockSpec((1, d), ij(lambda i, j: (0, 0))))
    for _, layer, c0 in weights:
        in_specs.append(pl.BlockSpec((None, d, bn), ij(lambda i, j, layer=layer, c0=c0: (layer, 0, j + c0))))
    in_specs += [pl.BlockSpec(shape, ij(fn)) for _, shape, fn in extras]
    in_specs += [pl.BlockSpec(shape, step(fn)) for _, shape, fn in side_ins]
    out_specs = [pl.BlockSpec(shape, ij(fn)) for _, shape, fn in outs]
    out_specs += [pl.BlockSpec(shape, step(fn)) for _, shape, fn in side_outs]
    scratch = []
    if cast_weights:
        scratch += [pltpu.VMEM((d, bn), BF16) for _ in weights]
    if prenorm:
        scratch.append(pltpu.VMEM((bm, d), BF16))
    scratch += list(epi_scratch)
    kern = functools.partial(_up_kernel, n_w=len(weights), n_extra=len(extras), n_out=len(outs),
                             side_jobs=[(len(ins), len(outs_), fn) for ins, outs_, fn in built],
                             weight_stationary=weight_stationary, cast_weights=cast_weights,
                             row_subtiles=row_subtiles, prenorm=prenorm, epilogue=epilogue)
    res = pl.pallas_call(
        kern,
        grid=grid,
        in_specs=in_specs,
        out_specs=out_specs,
        out_shape=[sds for sds, _, _ in outs] + [sds for sds, _, _ in side_outs],
        scratch_shapes=scratch,
        compiler_params=_params("arbitrary", "arbitrary"),
        name=name,
    )(h, *([prenorm_gain.reshape(1, d)] if prenorm else []), *[w for w, _, _ in weights],
      *[a for a, _, _ in extras], *[a for a, _, _ in side_ins])
    return res[:len(outs)], res[len(outs):]


def _cast_rows_side(w, layer, *, n_used=None, keep_layer_axis=False):
    kdim, d = w.shape[1:]

    def build(n_steps):
        used = n_steps if n_used is None else n_used
        rows = kdim // used
        assert used <= n_steps and rows * used == kdim and rows % (2 * SUBLANES) == 0, (kdim, used)

        def fn(in_refs, out_refs):
            out_refs[0][...] = in_refs[0][...].astype(BF16)

        def block(t):
            return jnp.minimum(t, used - 1)

        if keep_layer_axis:
            out = (jax.ShapeDtypeStruct((1, kdim, d), BF16), (None, rows, d), lambda t: (0, block(t), 0))
        else:
            out = (jax.ShapeDtypeStruct((kdim, d), BF16), (rows, d), lambda t: (block(t), 0))
        return [(w, (None, rows, d), lambda t: (layer, block(t), 0))], [out], fn

    return build


def _lane_partial_sum(y):
    acc = y[:, 0:LANES]
    for c in range(1, y.shape[1] // LANES):
        acc = acc + y[:, c * LANES:(c + 1) * LANES]
    return acc


def _gelu_exact(y):
    return 0.5 * y * (1.0 + lax.erf(y * np.float32(np.sqrt(0.5))))


def _gmlp_in_epilogue(ys, extras, outs, scratch, ip, jp, *, n_u_tiles, n_tiles, v_width):
    z_ref, mu_ref, rstd_ref = outs
    s_ref, ss_ref = scratch
    z = _gelu_exact(ys[0])
    z_ref[...] = z.astype(z_ref.dtype)
    carry = jp > n_u_tiles
    in_v = jp >= n_u_tiles
    s_ref[...] = jnp.where(carry, s_ref[...], 0.0) + jnp.where(in_v, _lane_partial_sum(z), 0.0)
    ss_ref[...] = jnp.where(carry, ss_ref[...], 0.0) + jnp.where(in_v, _lane_partial_sum(z * z), 0.0)

    @pl.when(jp == n_tiles - 1)
    def _():
        mu = jnp.sum(s_ref[...], axis=-1, keepdims=True) / v_width
        var = jnp.sum(ss_ref[...], axis=-1, keepdims=True) / v_width - mu * mu
        mu_ref[...] = mu
        rstd_ref[...] = lax.rsqrt(var + EPS)


def _gmlp_in(x, pre_g, w, w_out, *, bm, bn):
    h = x
    m = h.shape[0]
    n = w.shape[2]
    n_tiles = n // bn
    epilogue = functools.partial(_gmlp_in_epilogue, n_u_tiles=n_tiles // 2, n_tiles=n_tiles,
                                 v_width=n // 2)
    (z, mu, rstd), (w_out_bf,) = _up_proj(
        h, [(w, 0, 0)], n_col_tiles=n_tiles, bm=bm, bn=bn, weight_stationary=False, epilogue=epilogue,
        outs=[(jax.ShapeDtypeStruct((m, n), BF16), (bm, bn), lambda i, j: (i, j)),
              (jax.ShapeDtypeStruct((m, 1), F32), (bm, 1), lambda i, j: (i, 0)),
              (jax.ShapeDtypeStruct((m, 1), F32), (bm, 1), lambda i, j: (i, 0))],
        epi_scratch=[pltpu.VMEM((bm, LANES), F32), pltpu.VMEM((bm, LANES), F32)],
        sides=[_cast_rows_side(w_out, 0)], prenorm_gain=pre_g, name="gmlp_in")
    return z, mu, rstd, w_out_bf


def _causal_conv(z, prev, cw, cb):
    row = lax.broadcasted_iota(jnp.int32, z.shape, 0)
    z1 = jnp.where(row == 0, prev[7:8, :], pltpu.roll(z, 1, 0))
    z2 = jnp.where(row == 0, prev[6:7, :], jnp.where(row == 1, prev[7:8, :], pltpu.roll(z, 2, 0)))
    return cb + cw[0:1, :] * z2 + cw[1:2, :] * z1 + cw[2:3, :] * z


def _ffn_up_epilogue(ys, extras, outs, scratch, ip, jp, *, tiles_per_seq):
    cwg_ref, cwu_ref, cbg_ref, cbu_ref = extras
    seq_start = ip % tiles_per_seq == 0

    def branch(z, cw_ref, cb_ref, carry_ref):
        prev = jnp.where(seq_start, 0.0, carry_ref[...])
        out = _causal_conv(z, prev, cw_ref[...], cb_ref[...])
        carry_ref[...] = z[z.shape[0] - SUBLANES:, :]
        return out

    gate = branch(ys[0], cwg_ref, cbg_ref, scratch[0])
    up = branch(ys[1], cwu_ref, cbu_ref, scratch[1])
    outs[0][...] = (jax.nn.silu(gate) * up).astype(outs[0].dtype)


def _ffn_up(h, w, conv_w, conv_b, w_down, layer, *, seq, bm, tn, row_subtiles, more_sides=()):
    m = h.shape[0]
    f = w.shape[2] // 2
    nj = f // tn
    cb = conv_b.reshape(conv_b.shape[0], 1, 2 * f)
    (a,), side_outs = _up_proj(
        h, [(w, layer, 0), (w, layer, nj)], n_col_tiles=nj, bm=bm, bn=tn, weight_stationary=True,
        epilogue=functools.partial(_ffn_up_epilogue, tiles_per_seq=seq * row_subtiles // bm),
        extras=[(conv_w, (None, CONV_WIDTH, tn), lambda i, j: (layer, 0, j)),
                (conv_w, (None, CONV_WIDTH, tn), lambda i, j: (layer, 0, j + nj)),
                (cb, (None, 1, tn), lambda i, j: (layer, 0, j)),
                (cb, (None, 1, tn), lambda i, j: (layer, 0, j + nj))],
        outs=[(jax.ShapeDtypeStruct((m, f), BF16), (bm, tn), lambda i, j: (i, j))],
        epi_scratch=[pltpu.VMEM((SUBLANES, tn), F32), pltpu.VMEM((SUBLANES, tn), F32)],
        sides=[_cast_rows_side(w_down, layer), *more_sides], row_subtiles=row_subtiles, name="ffn_up")
    return (a, *side_outs)


def _rope_epilogue(ys, extras, outs, scratch, ip, jp, *, n_q_tiles, head_qk):
    cos = extras[0][...]
    sin = extras[1][...]
    y, o_ref = ys[0], outs[0]
    even = (lax.broadcasted_iota(jnp.int32, cos.shape, 1) & 1) == 0
    scale = jnp.where(jp >= n_q_tiles, head_qk ** -0.5, 1.0).astype(F32)
    for hh in range(y.shape[1] // head_qk):
        t = y[:, hh * head_qk:(hh + 1) * head_qk]
        nxt = pltpu.roll(t, head_qk - 1, 1)
        prv = pltpu.roll(t, 1, 1)
        r = (t * cos + jnp.where(even, -nxt, prv) * sin) * scale
        o_ref[hh] = r.astype(o_ref.dtype)


def _value_gate_epilogue(ys, extras, outs, scratch, ip, jp, *, n_v_tiles):
    o_ref = outs[0]
    head_v = o_ref.shape[2]
    y = ys[0]
    y = jnp.where(jp >= n_v_tiles, jax.nn.silu(y), y).astype(o_ref.dtype)
    for hh in range(o_ref.shape[0]):
        o_ref[hh] = y[:, hh * head_v:(hh + 1) * head_v]


def _ret_in(h, w, cos, sin, w_out, *, bm, bn, qk_dim, v_dim, head_qk):
    m = h.shape[0]
    nq = qk_dim // bn
    nv = v_dim // bn
    head_v = v_dim // RET_HEADS
    tab = (bm, head_qk)
    qk_heads = bn // head_qk
    vg_heads = bn // head_v
    (qk,), _ = _up_proj(
        h, [(w, 0, 0)], n_col_tiles=2 * nq, bm=bm, bn=bn, weight_stationary=True,
        epilogue=functools.partial(_rope_epilogue, n_q_tiles=nq, head_qk=head_qk),
        extras=[(cos, tab, lambda i, j: (i, 0)), (sin, tab, lambda i, j: (i, 0))],
        outs=[(jax.ShapeDtypeStruct((2 * RET_HEADS, m, head_qk), BF16), (qk_heads, bm, head_qk),
               lambda i, j: (j, i, 0))],
        name="ret_in_qk")
    (vg,), (w_out_bf,) = _up_proj(
        h, [(w, 0, 2 * nq)], n_col_tiles=2 * nv, bm=bm, bn=bn, weight_stationary=True,
        epilogue=functools.partial(_value_gate_epilogue, n_v_tiles=nv),
        outs=[(jax.ShapeDtypeStruct((2 * RET_HEADS, m, head_v), BF16), (vg_heads, bm, head_v),
               lambda i, j: (j, i, 0))],
        sides=[_cast_rows_side(w_out, 0)], name="ret_in_vg")
    return qk, vg, w_out_bf


def _residual_norm_epilogue(acc, x_ref, pg_ref, ng_ref, xo_ref, ho_ref):
    xn = x_ref[...] + _rms(acc) * pg_ref[...]
    xo_ref[...] = xn
    if ho_ref is not None:
        ho_ref[...] = (_rms(xn) * ng_ref[...]).astype(ho_ref.dtype)


def _down_kernel(a_ref, w_ref, x_ref, pg_ref, *rest, with_next):
    if with_next:
        ng_ref, xo_ref, ho_ref, raw0_ref, raw1_ref = rest
    else:
        (xo_ref, raw0_ref, raw1_ref), ng_ref, ho_ref = rest, None, None
    raws = (raw0_ref, raw1_ref)

    def matmul(par):
        raws[par][...] = jnp.dot(a_ref[...], w_ref[...], preferred_element_type=F32)

    def finish(par):
        _residual_norm_epilogue(raws[par][...], x_ref, pg_ref, ng_ref, xo_ref, ho_ref)

    _staged_pipeline(pl.program_id(0), pl.num_programs(0) - 1, [matmul, finish])


def _down_proj(a, w, x, post_g, next_g=None, *, bm):
    m, kdim = a.shape
    d = w.shape[1]
    n = m // bm
    with_next = next_g is not None
    prev_row = pl.BlockSpec((bm, d), lambda s: (jnp.maximum(s - 1, 0), 0))
    gain = pl.BlockSpec((1, d), lambda s: (0, 0))
    gains = [post_g.reshape(1, d)] + ([next_g.reshape(1, d)] if with_next else [])
    out_shape = [jax.ShapeDtypeStruct((m, d), F32)]
    if with_next:
        out_shape.append(jax.ShapeDtypeStruct((m, d), BF16))
    res = pl.pallas_call(
        functools.partial(_down_kernel, with_next=with_next),
        grid=(n + 1,),
        in_specs=[
            pl.BlockSpec((bm, kdim), lambda s: (jnp.minimum(s, n - 1), 0)),
            pl.BlockSpec((kdim, d), lambda s: (0, 0)),
            prev_row,
        ] + [gain] * len(gains),
        out_specs=[prev_row] * len(out_shape),
        out_shape=out_shape,
        scratch_shapes=[pltpu.VMEM((bm, d), F32), pltpu.VMEM((bm, d), F32)],
        compiler_params=_params("arbitrary"),
        name="down_proj",
    )(a, w, x, *gains)
    return res if with_next else (res[0], None)


def _gmlp_down_kernel(u_ref, v_ref, mu_ref, rstd_ref, lg_ref, lb_ref, ws_ref, bs_ref, w_ref,
                      x_ref, pg_ref, ng_ref, xo_ref, ho_ref, raw0_ref, raw1_ref):
    raws = (raw0_ref, raw1_ref)
    bm, half = u_ref.shape
    gd = half // GMLP_GROUPS

    def gate_and_project(par):
        tril = (lax.broadcasted_iota(jnp.int32, (CHUNK, CHUNK), 0)
                >= lax.broadcasted_iota(jnp.int32, (CHUNK, CHUNK), 1))
        mu = mu_ref[...]
        rstd = rstd_ref[...]
        acc = None
        for g in range(GMLP_GROUPS):
            cols = slice(g * gd, (g + 1) * gd)
            ws = jnp.where(tril, ws_ref[g], 0.0).astype(BF16)
            bias = bs_ref[g]
            vn = (v_ref[:, cols].astype(F32) - mu) * rstd * lg_ref[:, cols] + lb_ref[:, cols]
            vn = vn.astype(BF16)
            gated = []
            for c in range(bm // CHUNK):
                rows = slice(c * CHUNK, (c + 1) * CHUNK)
                mixed = jnp.dot(ws, vn[rows], preferred_element_type=F32) + bias
                gated.append((u_ref[rows, cols].astype(F32) * mixed).astype(BF16))
            part = jnp.dot(jnp.concatenate(gated, axis=0), w_ref[cols, :], preferred_element_type=F32)
            acc = part if acc is None else acc + part
        raws[par][...] = acc

    def finish(par):
        _residual_norm_epilogue(raws[par][...], x_ref, pg_ref, ng_ref, xo_ref, ho_ref)

    _staged_pipeline(pl.program_id(0), pl.num_programs(0) - 1, [gate_and_project, finish])


def _gmlp_down(z, mu, rstd, ln_g, ln_b, w_s, b_s, w, x, post_g, next_g, *, bm):
    m = z.shape[0]
    half, d = w.shape
    n = m // bm

    def first(s):
        return jnp.minimum(s, n - 1)

    def last(s):
        return jnp.maximum(s - 1, 0)

    row = pl.BlockSpec((bm, d), lambda s: (last(s), 0))
    gain = pl.BlockSpec((1, d), lambda s: (0, 0))
    stat = pl.BlockSpec((bm, 1), lambda s: (first(s), 0))
    lnp = pl.BlockSpec((1, half), lambda s: (0, 0))
    return pl.pallas_call(
        _gmlp_down_kernel,
        grid=(n + 1,),
        in_specs=[
            pl.BlockSpec((bm, half), lambda s: (first(s), 0)),
            pl.BlockSpec((bm, half), lambda s: (first(s), 1)),
            stat, stat, lnp, lnp,
            pl.BlockSpec((GMLP_GROUPS, CHUNK, CHUNK), lambda s: (0, 0, 0)),
            pl.BlockSpec((GMLP_GROUPS, CHUNK, 1), lambda s: (0, 0, 0)),
            pl.BlockSpec((half, d), lambda s: (0, 0)),
            row, gain, gain,
        ],
        out_specs=[row, row],
        out_shape=[jax.ShapeDtypeStruct((m, d), F32), jax.ShapeDtypeStruct((m, d), BF16)],
        scratch_shapes=[pltpu.VMEM((bm, d), F32), pltpu.VMEM((bm, d), F32)],
        compiler_params=_params("arbitrary"),
        name="gmlp_down",
    )(z, z, mu, rstd, ln_g.reshape(1, half), ln_b.reshape(1, half), w_s,
      b_s.reshape(GMLP_GROUPS, CHUNK, 1), w, x, post_g.reshape(1, d), next_g.reshape(1, d))


def _rope_table_kernel(pos_ref, inv_ref, w_ref, cos_ref, sin_ref, wo_ref):
    ang = pos_ref[...].astype(F32) * inv_ref[...]
    cos_ref[...] = jnp.cos(ang)
    sin_ref[...] = jnp.sin(ang)
    wo_ref[...] = w_ref[...].astype(BF16)


def _rope_tables(pos, inv_freq, w, *, bm):
    m = pos.shape[0]
    dk = inv_freq.shape[1]
    n_steps = m // bm
    layers, kdim, n = w.shape
    wrows = kdim // n_steps
    assert wrows * n_steps == kdim and wrows % (2 * SUBLANES) == 0, (kdim, n_steps)
    tab = pl.BlockSpec((bm, dk), lambda i: (i, 0))
    wblk = pl.BlockSpec((layers, wrows, n), lambda i: (0, i, 0))
    return pl.pallas_call(
        _rope_table_kernel,
        grid=(n_steps,),
        in_specs=[pl.BlockSpec((bm, 1), lambda i: (i, 0)), pl.BlockSpec((1, dk), lambda i: (0, 0)), wblk],
        out_specs=[tab, tab, wblk],
        out_shape=[jax.ShapeDtypeStruct((m, dk), F32)] * 2 + [jax.ShapeDtypeStruct(w.shape, BF16)],
        compiler_params=_params("arbitrary"),
        name="rope_tables",
    )(pos, inv_freq, w)


def _ret_core_kernel(q_ref, k_ref, v_ref, g_ref, dm_ref, qd_ref, kd_ref, cd_ref, o_ref):
    dmask = dm_ref[0]
    qdec = qd_ref[0]
    kdec = kd_ref[0]
    cdec = cd_ref[0]
    chunk = dmask.shape[0]
    state = jnp.zeros((q_ref.shape[1], v_ref.shape[1]), F32)
    for c in range(q_ref.shape[0] // chunk):
        rows = slice(c * chunk, (c + 1) * chunk)
        qc = q_ref[rows, :]
        kc = k_ref[rows, :]
        vc = v_ref[rows, :]
        scores = lax.dot_general(qc, kc, (((1,), (1,)), ((), ())), preferred_element_type=F32) * dmask
        intra = jnp.dot(scores.astype(BF16), vc, preferred_element_type=F32)
        qd = (qc.astype(F32) * qdec).astype(BF16)
        cross = jnp.dot(qd, state.astype(BF16), preferred_element_type=F32)
        kd = (kc.astype(F32) * kdec).astype(BF16)
        upd = lax.dot_general(kd, vc, (((0,), (0,)), ((), ())), preferred_element_type=F32)
        state = state * cdec + upd
        o = _rms(intra + cross)
        o_ref[rows, :] = (g_ref[rows, :].astype(F32) * o).astype(o_ref.dtype)


def _ret_core(qk, vg, dmask, qdec, kdec, cdec, *, batch, seq, stream_buffers):
    m, hq = qk.shape[1:]
    hv = vg.shape[2]
    chunk = dmask.shape[1]
    deep = pl.Buffered(stream_buffers)
    in_specs = [
        pl.BlockSpec((None, seq, hq), lambda b, h: (h, b, 0), pipeline_mode=deep),
        pl.BlockSpec((None, seq, hq), lambda b, h: (RET_HEADS + h, b, 0), pipeline_mode=deep),
        pl.BlockSpec((None, seq, hv), lambda b, h: (h, b, 0), pipeline_mode=deep),
        pl.BlockSpec((None, seq, hv), lambda b, h: (RET_HEADS + h, b, 0), pipeline_mode=deep),
        pl.BlockSpec((1, chunk, chunk), lambda b, h: (h, 0, 0)),
        pl.BlockSpec((1, chunk, 1), lambda b, h: (h, 0, 0)),
        pl.BlockSpec((1, chunk, 1), lambda b, h: (h, 0, 0)),
        pl.BlockSpec((1, 1, 1), lambda b, h: (h, 0, 0)),
    ]
    out_specs = [pl.BlockSpec((seq, hv), lambda b, h: (b, h))]

    def pipelined(*refs):
        pltpu.emit_pipeline(_ret_core_kernel, grid=(batch, RET_HEADS), in_specs=in_specs,
                            out_specs=out_specs)(*refs)

    in_hbm = pl.BlockSpec(memory_space=pl.ANY)
    return pl.pallas_call(
        pipelined,
        in_specs=[in_hbm] * len(in_specs),
        out_specs=[in_hbm] * len(out_specs),
        out_shape=[jax.ShapeDtypeStruct((m, hv * RET_HEADS), BF16)],
        compiler_params=pltpu.CompilerParams(vmem_limit_bytes=VMEM_LIMIT_BYTES),
        name="ret_core",
    )(qk, qk, vg, vg, dmask, qdec, kdec, cdec)[0]


def _retention_constants(head_qk, chunk):
    log_gamma = jnp.log1p(-jnp.exp2(-5.0 - jnp.arange(RET_HEADS, dtype=F32)))
    idx = jnp.arange(chunk, dtype=F32)
    rel = idx[:, None] - idx[None, :]
    dmask = jnp.where(rel[None] >= 0,
                      jnp.exp(jnp.maximum(rel, 0.0)[None] * log_gamma[:, None, None]), 0.0)
    qdec = jnp.exp((idx[None, :] + 1.0) * log_gamma[:, None])[:, :, None]
    kdec = jnp.exp((chunk - 1.0 - idx)[None, :] * log_gamma[:, None])[:, :, None]
    cdec = jnp.exp(chunk * log_gamma)[:, None, None]
    inv_freq = 1.0 / (ROPE_BASE ** jnp.linspace(0.0, 1.0, head_qk // 2, dtype=F32))
    inv_freq = jnp.repeat(inv_freq, 2)[None, :]
    return dmask, qdec, kdec, cdec, inv_freq


def kernel(x, positions, mix_pre_g, mix_post_g, gmlp_w_in, gmlp_ln_g, gmlp_ln_b, gmlp_w_s, gmlp_b_s,
           gmlp_w_out, ret_w_in, ret_w_out, ffn_pre_g, ffn_post_g, ffn_w_up, ffn_conv_w, ffn_conv_b,
           ffn_w_down):
    batch, seq, d = x.shape
    m = batch * seq
    qk_dim = d
    v_dim = (ret_w_in.shape[2] - 2 * qk_dim) // 2
    head_qk = qk_dim // RET_HEADS
    xf = x.reshape(m, d)

    dmask, qdec, kdec, cdec, inv_freq = _retention_constants(head_qk, RET_CHUNK)
    cos, sin, gmlp_w_in_bf = _rope_tables(positions.reshape(m, 1), inv_freq, gmlp_w_in, bm=1024)

    z, mu, rstd, w_bf = _gmlp_in(xf, mix_pre_g[0], gmlp_w_in_bf, gmlp_w_out, bm=1024, bn=2048)
    xf, h = _gmlp_down(z, mu, rstd, gmlp_ln_g[0], gmlp_ln_b[0], gmlp_w_s[0], gmlp_b_s[0],
                       w_bf, xf, mix_post_g[0], ffn_pre_g[0], bm=256)
    ret_w_in_side = _cast_rows_side(ret_w_in, 0, n_used=64, keep_layer_axis=True)
    a, w_bf, ret_w_in_bf = _ffn_up(h, ffn_w_up, ffn_conv_w, ffn_conv_b, ffn_w_down, 0, seq=seq, bm=2048,
                                   tn=512, row_subtiles=2, more_sides=[ret_w_in_side])
    xf, h = _down_proj(a, w_bf, xf, ffn_post_g[0], mix_pre_g[1], bm=256)

    qk, vg, w_bf = _ret_in(h, ret_w_in_bf, cos, sin, ret_w_out, bm=1024, bn=2048,
                           qk_dim=qk_dim, v_dim=v_dim, head_qk=head_qk)
    o = _ret_core(qk, vg, dmask, qdec, kdec, cdec, batch=batch, seq=seq, stream_buffers=3)
    xf, h = _down_proj(o, w_bf, xf, mix_post_g[1], ffn_pre_g[1], bm=256)
    a, w_bf = _ffn_up(h, ffn_w_up, ffn_conv_w, ffn_conv_b, ffn_w_down, 1, seq=seq, bm=2048, tn=512,
                      row_subtiles=2)
    xf, _ = _down_proj(a, w_bf, xf, ffn_post_g[1], bm=256)
    return xf.reshape(batch, seq, d)
```
